```python
import math
import jax, jax.numpy as jnp
from jax import lax
import numpy as np

D_MODEL = 4096
BATCH = 4
SEQ = 2048
DEPTH = 2
DEC_BATCH = 128
DEC_SEQ = 4
PAST_LEN = 16384
PAGE_SIZE = 128

GLA_HEADS = 4
GLA_DK = D_MODEL // 2 // GLA_HEADS
GLA_DV = D_MODEL // GLA_HEADS
GLA_RANK = 16
GLA_GATE_NORM = 16.0
GLA_CHUNK = 64
HG_EXPAND = 128
HG_HEADS = D_MODEL // HG_EXPAND
HG_DV = D_MODEL // HG_HEADS
HG_CHUNK = 32
D_FF = ((8 * D_MODEL // 3 + 255) // 256) * 256
CONV_W = 3
DN_ALPHA = (2.0 * DEPTH) ** 0.25
DN_BETA = (8.0 * DEPTH) ** -0.25
EPS = 1e-5

GLA_QK = GLA_HEADS * GLA_DK
GLA_V = GLA_HEADS * GLA_DV
HG_K = HG_HEADS * HG_EXPAND
HG_V = HG_HEADS * HG_DV
SPLIT_SIZES = (GLA_QK, GLA_QK, GLA_V, GLA_V, GLA_RANK, HG_K, HG_K, HG_V, HG_V, D_MODEL, D_MODEL)
IN_WIDTH = sum(SPLIT_SIZES)

kernel_name = "gla_hgrn2_gated_merge_convffn_deepnorm_step"


def _layer_norm(x, g, b):
    xf = x.astype(jnp.float32)
    mu = jnp.mean(xf, axis=-1, keepdims=True)
    var = jnp.mean(jnp.square(xf - mu), axis=-1, keepdims=True)
    y = (xf - mu) * lax.rsqrt(var + EPS) * g.astype(jnp.float32) + b.astype(jnp.float32)
    return y.astype(x.dtype)


def _rms_head(o, g):
    of = o.astype(jnp.float32)
    y = of * lax.rsqrt(jnp.mean(jnp.square(of), axis=-1, keepdims=True) + EPS) * g.astype(jnp.float32)
    return y


def _chunked_gated_linear(q, k, v, log_g, s0, chunk):
    B, T, H, dk = q.shape
    dv = v.shape[-1]
    c = chunk if T % chunk == 0 else T
    n = T // c
    f32 = jnp.float32
    q, k, log_g = (a.astype(f32).reshape(B, n, c, H, dk) for a in (q, k, log_g))
    v = v.astype(f32).reshape(B, n, c, H, dv)
    b = jnp.cumsum(log_g, axis=2)
    b_ref = b[:, :, c // 2][:, :, None]
    b_last = b[:, :, -1]
    scores = jnp.einsum('bnthk,bnshk->bnhts', q * jnp.exp(b - b_ref), k * jnp.exp(b_ref - b))
    causal = jnp.tril(jnp.ones((c, c), dtype=bool))
    scores = jnp.where(causal, scores, 0.0)
    o_intra = jnp.einsum('bnhts,bnshv->bnthv', scores, v)
    q_in = q * jnp.exp(b)
    k_out = k * jnp.exp(b_last[:, :, None] - b)
    decay = jnp.exp(b_last)

    def step(S, xs):
        qc, kc, vc, dc = xs
        o = jnp.einsum('bthk,bhkv->bthv', qc, S)
        S = dc[..., None] * S + jnp.einsum('bshk,bshv->bhkv', kc, vc)
        return S, o

    mv = lambda a: jnp.moveaxis(a, 1, 0)
    S, o_inter = lax.scan(step, s0.astype(f32), (mv(q_in), mv(k_out), mv(v), mv(decay)))
    o = o_intra + jnp.moveaxis(o_inter, 0, 1)
    return o.reshape(B, T, H, dv), S


def _trunk(x, s_gla, s_hg, s_conv, w_in, w_gla_gate2, b_gla_gate, g_gla_norm, lower_bounds,
           g_hgrn_norm, w_out, ln1_g, ln1_b, w_up, conv_w, conv_b, w_down, ln2_g, ln2_b):
    B, T, _ = x.shape
    f32 = jnp.float32
    lb_all = jnp.cumsum(jax.nn.softmax(lower_bounds.astype(f32), axis=0), axis=0)
    lb_all = lb_all - lb_all[:1]
    split_idx = np.cumsum(SPLIT_SIZES)[:-1]
    new_gla, new_hg, new_conv = [], [], []
    for l in range(DEPTH):
        p = jnp.einsum('btd,de->bte', x, w_in[l])
        (a_q, a_k, a_v, a_r, a_lr, h_q, h_f, h_i, h_g, m_a, m_b) = jnp.split(p, split_idx, axis=-1)
        q = a_q.reshape(B, T, GLA_HEADS, GLA_DK) * (GLA_DK ** -0.5)
        k = a_k.reshape(B, T, GLA_HEADS, GLA_DK)
        v = a_v.reshape(B, T, GLA_HEADS, GLA_DV)
        g_logit = jnp.einsum('btr,rk->btk', a_lr, w_gla_gate2[l]) + b_gla_gate[l]
        log_alpha = (jax.nn.log_sigmoid(g_logit.astype(f32)) / GLA_GATE_NORM).reshape(B, T, GLA_HEADS, GLA_DK)
        o_a, S_a = _chunked_gated_linear(q, k, v, log_alpha, s_gla[l], GLA_CHUNK)
        o_a = (_rms_head(o_a, g_gla_norm[l]).reshape(B, T, GLA_V) * jax.nn.silu(a_r.astype(f32)))
        lb = lb_all[l]
        zf = h_f.astype(f32)
        log_f = jnp.logaddexp(jnp.log(lb), jnp.log1p(-lb) + jax.nn.log_sigmoid(zf))
        k_h = (1.0 - lb) * jax.nn.sigmoid(-zf)
        q_h = jax.nn.silu(h_q.astype(f32)) * (HG_EXPAND ** -0.5)
        o_b, S_b = _chunked_gated_linear(q_h.reshape(B, T, HG_HEADS, HG_EXPAND),
                                         k_h.reshape(B, T, HG_HEADS, HG_EXPAND),
                                         h_i.reshape(B, T, HG_HEADS, HG_DV),
                                         log_f.reshape(B, T, HG_HEADS, HG_EXPAND),
                                         s_hg[l], HG_CHUNK)
        o_b = (_rms_head(o_b, g_hgrn_norm[l]).reshape(B, T, HG_V) * jax.nn.silu(h_g.astype(f32)))
        merged = (jax.nn.sigmoid(m_a.astype(f32)) * o_a + jax.nn.sigmoid(m_b.astype(f32)) * o_b).astype(x.dtype)
        mix = jnp.einsum('bte,ed->btd', merged, w_out[l])
        x = _layer_norm(DN_ALPHA * x + mix, ln1_g[l], ln1_b[l])
        u = jnp.einsum('btd,df->btf', x, w_up[l])
        cat = jnp.concatenate([s_conv[l].astype(u.dtype), u], axis=1)
        uc = sum(cat[:, j:j + T] * conv_w[l, j] for j in range(CONV_W)) + conv_b[l]
        a_ff, g_ff = jnp.split(uc, 2, axis=-1)
        h = jax.nn.silu(a_ff) * g_ff
        y = jnp.einsum('btf,fd->btd', h, w_down[l])
        x = _layer_norm(DN_ALPHA * x + y, ln2_g[l], ln2_b[l])
        new_gla.append(S_a.astype(s_gla.dtype))
        new_hg.append(S_b.astype(s_hg.dtype))
        new_conv.append(cat[:, -(CONV_W - 1):].astype(s_conv.dtype))
    return x, jnp.stack(new_gla), jnp.stack(new_hg), jnp.stack(new_conv)


def setup_inputs(seed: int = 0) -> dict:
    key = jax.random.key(seed)
    ks = jax.random.split(key, 24)
    nrm = jax.random.normal
    col_scale = jnp.concatenate([
        jnp.ones((2 * GLA_QK,)), jnp.full((GLA_V,), DN_BETA), jnp.ones((GLA_V + GLA_RANK + 2 * HG_K,)),
        jnp.full((HG_V,), DN_BETA), jnp.ones((HG_V + 2 * D_MODEL,))]).astype(jnp.float32)
    return {
        "x_prompt": nrm(ks[0], (BATCH, SEQ, D_MODEL), jnp.float32),
        "x_sample": nrm(ks[1], (DEC_BATCH, DEC_SEQ, D_MODEL), jnp.float32),
        "state_gla": 0.5 * nrm(ks[2], (DEPTH, DEC_BATCH, GLA_HEADS, GLA_DK, GLA_DV), jnp.float32),
        "state_hgrn": 0.5 * nrm(ks[3], (DEPTH, DEC_BATCH, HG_HEADS, HG_EXPAND, HG_DV), jnp.float32),
        "state_ffn_conv": 0.5 * nrm(ks[4], (DEPTH, DEC_BATCH, CONV_W - 1, 2 * D_FF), jnp.float32),
        "w_in": nrm(ks[5], (DEPTH, D_MODEL, IN_WIDTH), jnp.float32) * (D_MODEL ** -0.5) * col_scale,
        "w_gla_gate2": nrm(ks[6], (DEPTH, GLA_RANK, GLA_QK), jnp.float32) * (GLA_RANK ** -0.5),
        "b_gla_gate": 0.1 * nrm(ks[7], (DEPTH, GLA_QK), jnp.float32),
        "g_gla_norm": 1.0 + 0.02 * nrm(ks[8], (DEPTH, GLA_DV), jnp.float32),
        "lower_bounds": 0.1 * nrm(ks[9], (DEPTH, HG_K), jnp.float32),
        "g_hgrn_norm": 1.0 + 0.02 * nrm(ks[10], (DEPTH, HG_DV), jnp.float32),
        "w_out": nrm(ks[11], (DEPTH, D_MODEL, D_MODEL), jnp.float32) * (D_MODEL ** -0.5) * DN_BETA,
        "ln1_g": 1.0 + 0.02 * nrm(ks[12], (DEPTH, D_MODEL), jnp.float32),
        "ln1_b": 0.02 * nrm(ks[13], (DEPTH, D_MODEL), jnp.float32),
        "w_up": nrm(ks[14], (DEPTH, D_MODEL, 2 * D_FF), jnp.float32) * (D_MODEL ** -0.5) * DN_BETA,
        "conv_w": nrm(ks[15], (DEPTH, CONV_W, 2 * D_FF), jnp.float32) * (CONV_W ** -0.5),
        "conv_b": 0.01 * nrm(ks[16], (DEPTH, 2 * D_FF), jnp.float32),
        "w_down": nrm(ks[17], (DEPTH, D_FF, D_MODEL), jnp.float32) * (D_FF ** -0.5) * DN_BETA,
        "ln2_g": 1.0 + 0.02 * nrm(ks[18], (DEPTH, D_MODEL), jnp.float32),
        "ln2_b": 0.02 * nrm(ks[19], (DEPTH, D_MODEL), jnp.float32),
    }


def reference(x_prompt, x_sample, state_gla, state_hgrn, state_ffn_conv, w_in, w_gla_gate2, b_gla_gate,
              g_gla_norm, lower_bounds, g_hgrn_norm, w_out, ln1_g, ln1_b, w_up, conv_w, conv_b, w_down,
              ln2_g, ln2_b):
    Bp = x_prompt.shape[0]
    zg = jnp.zeros((DEPTH, Bp, GLA_HEADS, GLA_DK, GLA_DV), x_prompt.dtype)
    zh = jnp.zeros((DEPTH, Bp, HG_HEADS, HG_EXPAND, HG_DV), x_prompt.dtype)
    zc = jnp.zeros((DEPTH, Bp, CONV_W - 1, 2 * D_FF), x_prompt.dtype)
    y_prompt, gla_p, hg_p, conv_p = _trunk(
        x_prompt, zg, zh, zc, w_in, w_gla_gate2, b_gla_gate, g_gla_norm, lower_bounds, g_hgrn_norm,
        w_out, ln1_g, ln1_b, w_up, conv_w, conv_b, w_down, ln2_g, ln2_b)
    y_sample, gla_s, hg_s, conv_s = _trunk(
        x_sample, state_gla, state_hgrn, state_ffn_conv, w_in, w_gla_gate2, b_gla_gate, g_gla_norm,
        lower_bounds, g_hgrn_norm, w_out, ln1_g, ln1_b, w_up, conv_w, conv_b, w_down, ln2_g, ln2_b)
    return (y_prompt, y_sample, gla_p, gla_s, hg_p, hg_s, conv_p, conv_s)
```

```python
import functools

import jax
import jax.numpy as jnp
from jax import lax
from jax.experimental import pallas as pl
from jax.experimental.pallas import tpu as pltpu

GLA_CHUNK = 64
HG_CHUNK = 32
GLA_GATE_NORM = 16.0
EPS = 1e-5
CONV_W = 3

LANES = 128
SUBLANES = 8
BF16_ROWS = 16
V7X_VMEM_BYTES = 64 * 1024 * 1024
VMEM_LIMIT = V7X_VMEM_BYTES - 8 * 1024 * 1024

F32 = jnp.float32
BF16 = jnp.bfloat16
HIGHEST = lax.Precision.HIGHEST
_NT = (((1,), (1,)), ((), ()))
_TN = (((0,), (0,)), ((), ()))


def _divisor(n, cap, mult):
    if n <= cap:
        return n
    best = None
    for d in range(mult, cap + 1, mult):
        if n % d == 0:
            best = d
    assert best is not None, (n, cap, mult)
    return best


def _params(n_axes):
    return pltpu.CompilerParams(dimension_semantics=("arbitrary",) * n_axes, vmem_limit_bytes=VMEM_LIMIT)


def _sigmoid(x):
    return 1.0 / (1.0 + jnp.exp(-x))


def _silu(x):
    return x * _sigmoid(x)


def _log_sigmoid(x):
    return jnp.minimum(x, 0.0) - jnp.log1p(jnp.exp(-jnp.abs(x)))


def _mm_kernel(x_ref, w_ref, o_ref):
    o_ref[...] = jnp.dot(x_ref[...], w_ref[...], preferred_element_type=F32)


def _matmul(x, w, *, tm_cap, tn_cap, name):
    m, k = x.shape
    n = w.shape[1]
    tm = _divisor(m, tm_cap, BF16_ROWS)
    tn = _divisor(n, tn_cap, LANES)
    return pl.pallas_call(
        _mm_kernel,
        grid=(n // tn, m // tm),
        in_specs=[pl.BlockSpec((tm, k), lambda j, i: (i, 0)),
                  pl.BlockSpec((k, tn), lambda j, i: (0, j))],
        out_specs=pl.BlockSpec((tm, tn), lambda j, i: (i, j)),
        out_shape=jax.ShapeDtypeStruct((m, n), F32),
        compiler_params=_params(2),
        name=name,
    )(x, w)


def _ln_kernel(x_ref, y_ref, g_ref, b_ref, o32_ref, o16_ref, *, alpha):
    z = alpha * x_ref[...] + y_ref[...]
    mu = jnp.mean(z, axis=-1, keepdims=True)
    zc = z - mu
    var = jnp.mean(zc * zc, axis=-1, keepdims=True)
    o = zc * lax.rsqrt(var + EPS) * g_ref[...] + b_ref[...]
    o32_ref[...] = o
    o16_ref[...] = o.astype(BF16)


def _residual_ln(x, y, g, b, alpha, name):
    n, d = x.shape
    tr = _divisor(n, 256, BF16_ROWS)
    row = pl.BlockSpec((tr, d), lambda i: (i, 0))
    vec = pl.BlockSpec((1, d), lambda i: (0, 0))
    return pl.pallas_call(
        functools.partial(_ln_kernel, alpha=alpha),
        grid=(n // tr,),
        in_specs=[row, row, vec, vec],
        out_specs=[row, row],
        out_shape=[jax.ShapeDtypeStruct((n, d), F32), jax.ShapeDtypeStruct((n, d), BF16)],
        compiler_params=_params(1),
        name=name,
    )(x, y, g.reshape(1, d), b.reshape(1, d))


def _chunk_prep(q, k, lg, c, groups):
    r = q.shape[0]
    shift = c.bit_length() - 1
    ri = lax.broadcasted_iota(jnp.int32, (r, r), 0)
    ci = lax.broadcasted_iota(jnp.int32, (r, r), 1)
    tri = ci <= ri
    grp = lax.broadcasted_iota(jnp.int32, (r, 1), 0) >> shift
    if groups > 1:
        tri = tri & ((ri >> shift) == (ci >> shift))
    b = jnp.dot(tri.astype(F32), lg, precision=HIGHEST, preferred_element_type=F32)
    b_ref = b[c // 2:c // 2 + 1]
    b_last = b[c - 1:c]
    for g in range(1, groups):
        b_ref = jnp.where(grp == g, b[g * c + c // 2:g * c + c // 2 + 1], b_ref)
        b_last = jnp.where(grp == g, b[g * c + c - 1:g * c + c], b_last)
    qs = (q * jnp.exp(b - b_ref)).astype(BF16)
    ks = (k * jnp.exp(b_ref - b)).astype(BF16)
    q_in = (q * jnp.exp(b)).astype(BF16)
    k_out = k * jnp.exp(b_last - b)
    return tri, grp, b, qs, ks, q_in, k_out


def _chunk_head(tri, grp, b, qs, ks, q_in, k_out, v, c, groups, s_in, s_out):
    dk = qs.shape[1]
    dv = v.shape[1]
    vb = v.astype(BF16)
    scores = lax.dot_general(qs, ks, _NT, preferred_element_type=F32)
    scores = jnp.where(tri, scores, 0.0)
    o = jnp.dot(scores.astype(BF16), vb, preferred_element_type=F32)
    ones = jnp.ones((SUBLANES, LANES), F32)
    row0 = lax.broadcasted_iota(jnp.int32, (SUBLANES, 1), 0) == 0
    for g in range(groups):
        og = jnp.dot(q_in, s_in(g)[...].astype(BF16), preferred_element_type=F32)
        o = o + (og if groups == 1 else jnp.where(grp == g, og, 0.0))
        kg = k_out if groups == 1 else jnp.where(grp == g, k_out, 0.0)
        upd = lax.dot_general(kg.astype(BF16), vb, _TN, preferred_element_type=F32)
        bl = jnp.where(row0, jnp.broadcast_to(b[g * c + c - 1:g * c + c], (SUBLANES, dk)), 0.0)
        dcol = jnp.exp(lax.dot_general(bl, ones, _TN, precision=HIGHEST, preferred_element_type=F32))
        for j in range(dv // LANES):
            sl = slice(j * LANES, (j + 1) * LANES)
            s_out(g)[:, sl] = s_in(g)[:, sl] * dcol + upd[:, sl]
    return o


def _rms(o, gain):
    return o * lax.rsqrt(jnp.mean(o * o, axis=-1, keepdims=True) + EPS) * gain


def _gla_kernel(q_ref, k_ref, v_ref, r_ref, ma_ref, lr_ref, wg_ref, bg_ref, gn_ref, *rest,
                c, groups, chunks, n_tblk, scale, has_state):
    if has_state:
        s0_ref, ya_ref, so_ref = rest
        s_in = lambda g: s0_ref.at[g, 0]
        s_out = lambda g: so_ref.at[g, 0]
    else:
        ya_ref, so_ref, s_scr = rest
        s_in = s_out = lambda g: s_scr.at[g]
        tb = pl.program_id(2)

        @pl.when(tb == 0)
        def _():
            s_scr[...] = jnp.zeros_like(s_scr)

    rows_per_chunk = groups * c

    def chunk(ci, carry):
        rows = pl.ds(pl.multiple_of(ci * rows_per_chunk, rows_per_chunk), rows_per_chunk)
        gate = jnp.dot(lr_ref[rows, :].astype(BF16), wg_ref[...], preferred_element_type=F32) + bg_ref[...]
        lg = _log_sigmoid(gate) * (1.0 / GLA_GATE_NORM)
        prep = _chunk_prep(q_ref[rows, :] * scale, k_ref[rows, :], lg, c, groups)
        o = _chunk_head(*prep, v_ref[rows, :], c, groups, s_in, s_out)
        y = _rms(o, gn_ref[...]) * _silu(r_ref[rows, :])
        ya_ref[rows, :] = _sigmoid(ma_ref[rows, :]) * y
        return carry

    lax.fori_loop(0, chunks, chunk, 0)

    if not has_state:
        @pl.when(tb == n_tblk - 1)
        def _():
            so_ref[0, 0] = s_scr[0]


def _gla_branch(p, lr, wg, bg, gn, state, *, layer, row0, nb, t, c, heads, dk, dv, col, name):
    has_state = state is not None
    if has_state:
        assert t == c and SUBLANES % c == 0
        groups, chunks = SUBLANES // c, 1
    else:
        groups, chunks = 1, min(4, t // c)
    rblk = groups * c * chunks
    n_tblk = t // (c * chunks)
    assert nb % groups == 0 and row0 % rblk == 0 and t % (c * chunks) == 0
    rb0 = row0 // rblk

    def rowspec(width, off):
        assert off % width == 0
        return pl.BlockSpec((rblk, width), lambda b, h, tb: (rb0 + b * n_tblk + tb, off // width + h))

    in_specs = [rowspec(dk, col["q"]), rowspec(dk, col["k"]), rowspec(dv, col["v"]), rowspec(dv, col["r"]),
                rowspec(dv, col["ma"]),
                pl.BlockSpec((rblk, LANES), lambda b, h, tb: (rb0 + b * n_tblk + tb, 0)),
                pl.BlockSpec((LANES, dk), lambda b, h, tb: (0, h)),
                pl.BlockSpec((1, dk), lambda b, h, tb: (0, h)),
                pl.BlockSpec((1, dv), lambda b, h, tb: (0, 0))]
    args = [p, p, p, p, p, lr, wg, bg, gn]
    state_spec = pl.BlockSpec((groups, 1, dk, dv), lambda b, h, tb: (b, h, 0, 0))
    scratch = []
    if has_state:
        in_specs.append(pl.BlockSpec((None, groups, 1, dk, dv), lambda b, h, tb: (layer, b, h, 0, 0)))
        args.append(state)
    else:
        scratch = [pltpu.VMEM((1, dk, dv), F32)]
    ya, s_new = pl.pallas_call(
        functools.partial(_gla_kernel, c=c, groups=groups, chunks=chunks, n_tblk=n_tblk,
                          scale=dk ** -0.5, has_state=has_state),
        grid=(nb // groups, heads, n_tblk),
        in_specs=in_specs,
        out_specs=[pl.BlockSpec((rblk, dv), lambda b, h, tb: (b * n_tblk + tb, h)), state_spec],
        out_shape=[jax.ShapeDtypeStruct((nb * t, heads * dv), F32),
                   jax.ShapeDtypeStruct((nb, heads, dk, dv), F32)],
        scratch_shapes=scratch,
        compiler_params=_params(3),
        name=name,
    )(*args)
    return ya, s_new


def _hg_kernel(hq_ref, hf_ref, hi_ref, hg_ref, mb_ref, ya_ref, lb_ref, gn_ref, *rest,
               layer, c, groups, chunks, n_tblk, heads, dk, dv, scale, has_state):
    if has_state:
        s0_ref, out_ref, so_ref = rest
        s_in = lambda h: (lambda g: s0_ref.at[g, h])
        s_out = lambda h: (lambda g: so_ref.at[g, h])
    else:
        out_ref, so_ref, s_scr = rest
        s_in = s_out = lambda h: (lambda g: s_scr.at[h])
        tb = pl.program_id(1)

        @pl.when(tb == 0)
        def _():
            s_scr[...] = jnp.zeros_like(s_scr)

    lbw = lb_ref[...]
    e = jnp.exp(lbw - jnp.max(lbw, axis=0, keepdims=True))
    sm = e / jnp.sum(e, axis=0, keepdims=True)
    cum = sm[0:1]
    first = cum
    for i in range(1, layer + 1):
        cum = cum + sm[i:i + 1]
    lb = cum - first
    log_lb = jnp.log(lb)
    log_1m = jnp.log1p(-lb)
    one_m = 1.0 - lb

    rows_per_chunk = groups * c

    def chunk(ci, carry):
        rows = pl.ds(pl.multiple_of(ci * rows_per_chunk, rows_per_chunk), rows_per_chunk)
        zf = hf_ref[rows, :]
        x2 = log_1m + _log_sigmoid(zf)
        lg = jnp.maximum(log_lb, x2) + jnp.log1p(jnp.exp(-jnp.abs(log_lb - x2)))
        kk = one_m * _sigmoid(-zf)
        qq = _silu(hq_ref[rows, :]) * scale
        tri, grp, b, qs, ks, q_in, k_out = _chunk_prep(qq, kk, lg, c, groups)
        for h in range(heads):
            sk = slice(h * dk, (h + 1) * dk)
            sv = slice(h * dv, (h + 1) * dv)
            o = _chunk_head(tri, grp, b[:, sk], qs[:, sk], ks[:, sk], q_in[:, sk], k_out[:, sk],
                            hi_ref[rows, sv], c, groups, s_in(h), s_out(h))
            y = _rms(o, gn_ref[...]) * _silu(hg_ref[rows, sv])
            out_ref[rows, sv] = (ya_ref[rows, sv] + _sigmoid(mb_ref[rows, sv]) * y).astype(BF16)
        return carry

    lax.fori_loop(0, chunks, chunk, 0)

    if not has_state:
        @pl.when(tb == n_tblk - 1)
        def _():
            so_ref[0] = s_scr[...]


def _hg_branch(p, ya, lower_bounds, gn, state, *, layer, row0, nb, t, c, heads, dk, dv, col, name):
    has_state = state is not None
    if has_state:
        assert t == c and BF16_ROWS % c == 0
        groups, chunks = BF16_ROWS // c, 1
    else:
        groups, chunks = 1, min(2, t // c)
    rblk = groups * c * chunks
    n_tblk = t // (c * chunks)
    width = heads * dk
    assert heads * dv == width and nb % groups == 0 and row0 % rblk == 0 and t % (c * chunks) == 0
    rb0 = row0 // rblk

    def rowspec(off):
        assert off % width == 0
        return pl.BlockSpec((rblk, width), lambda b, tb: (rb0 + b * n_tblk + tb, off // width))

    local = pl.BlockSpec((rblk, width), lambda b, tb: (b * n_tblk + tb, 0))
    depth = lower_bounds.shape[0]
    in_specs = [rowspec(col["hq"]), rowspec(col["hf"]), rowspec(col["hi"]), rowspec(col["hg"]),
                rowspec(col["mb"]), local,
                pl.BlockSpec((depth, width), lambda b, tb: (0, 0)),
                pl.BlockSpec((1, dv), lambda b, tb: (0, 0))]
    args = [p, p, p, p, p, ya, lower_bounds, gn]
    state_spec = pl.BlockSpec((groups, heads, dk, dv), lambda b, tb: (b, 0, 0, 0))
    scratch = []
    if has_state:
        in_specs.append(pl.BlockSpec((None, groups, heads, dk, dv), lambda b, tb: (layer, b, 0, 0, 0)))
        args.append(state)
    else:
        scratch = [pltpu.VMEM((heads, dk, dv), F32)]
    merged, s_new = pl.pallas_call(
        functools.partial(_hg_kernel, layer=layer, c=c, groups=groups, chunks=chunks, n_tblk=n_tblk,
                          heads=heads, dk=dk, dv=dv, scale=dk ** -0.5, has_state=has_state),
        grid=(nb // groups, n_tblk),
        in_specs=in_specs,
        out_specs=[local, state_spec],
        out_shape=[jax.ShapeDtypeStruct((nb * t, width), BF16),
                   jax.ShapeDtypeStruct((nb, heads, dk, dv), F32)],
        scratch_shapes=scratch,
        compiler_params=_params(2),
        name=name,
    )(*args)
    return merged, s_new


def _up_kernel(x_ref, wa_ref, wg_ref, cwa_ref, cwg_ref, cba_ref, cbg_ref, h0a_ref, h0g_ref, h1a_ref, h1g_ref,
               h_ref, taila_ref, tailg_ref, usa_ref, usg_ref, sa_scr, sg_scr,
               *, tm, n_prompt_tiles, tiles_per_seq, ts):
    i = pl.program_id(1)
    is_sample = i >= n_prompt_tiles
    fresh = jnp.logical_or(is_sample, (i % tiles_per_seq) == 0)
    x = x_ref[...]
    t_in_seq = lax.broadcasted_iota(jnp.int32, (tm, 1), 0) & (ts - 1)
    keep1 = jnp.logical_or(jnp.logical_not(is_sample), t_in_seq >= 1)
    keep2 = jnp.logical_or(jnp.logical_not(is_sample), t_in_seq >= 2)
    hist = SUBLANES

    def half(w_ref, cw_ref, cb_ref, h0_ref, h1_ref, tail_ref, us_ref, scr):
        u = jnp.dot(x, w_ref[...], preferred_element_type=F32)

        @pl.when(fresh)
        def _():
            scr[0:hist, :] = jnp.zeros((hist, scr.shape[1]), F32)

        scr[hist:hist + tm, :] = u
        u1 = jnp.where(keep1, scr[hist - 1:hist - 1 + tm, :], 0.0)
        u2 = jnp.where(keep2, scr[hist - 2:hist - 2 + tm, :], 0.0)
        u1 = u1 + jnp.where(is_sample, h1_ref[...], 0.0)
        u2 = u2 + jnp.where(is_sample, h0_ref[...], 0.0)
        uc = u2 * cw_ref[0:1, :] + u1 * cw_ref[1:2, :] + u * cw_ref[2:3, :] + cb_ref[...]
        tail = scr[tm:tm + hist, :]
        scr[0:hist, :] = tail
        tail_ref[0] = tail

        @pl.when(is_sample)
        def _():
            us_ref[...] = u

        return uc

    a = half(wa_ref, cwa_ref, cba_ref, h0a_ref, h1a_ref, taila_ref, usa_ref, sa_scr)
    g = half(wg_ref, cwg_ref, cbg_ref, h0g_ref, h1g_ref, tailg_ref, usg_ref, sg_scr)
    h_ref[...] = (_silu(a) * g).astype(BF16)


def _up_conv_gate(xb, w, cw, cb, h0, h1, *, n_prompt, t_prompt, ts, name):
    n, d = xb.shape
    f = w.shape[1] // 2
    n_sample = n - n_prompt
    tm = _divisor(_gcd(t_prompt, n_sample), 512, BF16_ROWS)
    assert tm % ts == 0 and ts & (ts - 1) == 0 and ts >= CONV_W - 1 and tm >= SUBLANES
    tn = _divisor(f, 256, LANES)
    ncb = f // tn
    npt = n_prompt // tm
    nrt = n // tm
    srow = lambda j, i: (jnp.maximum(i - npt, 0), j)
    srow_g = lambda j, i: (jnp.maximum(i - npt, 0), ncb + j)
    in_specs = [pl.BlockSpec((tm, d), lambda j, i: (i, 0)),
                pl.BlockSpec((d, tn), lambda j, i: (0, j)),
                pl.BlockSpec((d, tn), lambda j, i: (0, ncb + j)),
                pl.BlockSpec((CONV_W, tn), lambda j, i: (0, j)),
                pl.BlockSpec((CONV_W, tn), lambda j, i: (0, ncb + j)),
                pl.BlockSpec((1, tn), lambda j, i: (0, j)),
                pl.BlockSpec((1, tn), lambda j, i: (0, ncb + j)),
                pl.BlockSpec((tm, tn), srow), pl.BlockSpec((tm, tn), srow_g),
                pl.BlockSpec((tm, tn), srow), pl.BlockSpec((tm, tn), srow_g)]
    tail_spec = pl.BlockSpec((1, SUBLANES, tn), lambda j, i: (i, 0, j))
    out_specs = [pl.BlockSpec((tm, tn), lambda j, i: (i, j)), tail_spec, tail_spec,
                 pl.BlockSpec((tm, tn), srow), pl.BlockSpec((tm, tn), srow)]
    out_shape = [jax.ShapeDtypeStruct((n, f), BF16),
                 jax.ShapeDtypeStruct((nrt, SUBLANES, f), F32), jax.ShapeDtypeStruct((nrt, SUBLANES, f), F32),
                 jax.ShapeDtypeStruct((n_sample, f), F32), jax.ShapeDtypeStruct((n_sample, f), F32)]
    return pl.pallas_call(
        functools.partial(_up_kernel, tm=tm, n_prompt_tiles=npt, tiles_per_seq=t_prompt // tm, ts=ts),
        grid=(ncb, nrt),
        in_specs=in_specs,
        out_specs=out_specs,
        out_shape=out_shape,
        scratch_shapes=[pltpu.VMEM((tm + SUBLANES, tn), F32), pltpu.VMEM((tm + SUBLANES, tn), F32)],
        compiler_params=_params(2),
        name=name,
    )(xb, w, w, cw, cw, cb, cb, h0, h0, h1, h1), tm


def _gcd(a, b):
    while b:
        a, b = b, a % b
    return a


def kernel(x_prompt, x_sample, state_gla, state_hgrn, state_ffn_conv, w_in, w_gla_gate2, b_gla_gate, g_gla_norm,
           lower_bounds, g_hgrn_norm, w_out, ln1_g, ln1_b, w_up, conv_w, conv_b, w_down, ln2_g, ln2_b):
    bp, tp, d = x_prompt.shape
    bs, ts, _ = x_sample.shape
    depth, _, heads, dk, dv = state_gla.shape
    _, _, hheads, hk, hv = state_hgrn.shape
    rank = w_gla_gate2.shape[1]
    f2 = w_up.shape[2]
    f = f2 // 2
    qk, vw, hgk, hgv = heads * dk, heads * dv, hheads * hk, hheads * hv
    c0 = 2 * qk + 2 * vw
    assert conv_w.shape[1] == CONV_W and rank <= LANES
    n_p, n_s = bp * tp, bs * ts
    alpha = (2.0 * depth) ** 0.25
    col = {"q": 0, "k": qk, "v": 2 * qk, "r": 2 * qk + vw,
           "hq": c0, "hf": c0 + hgk, "hi": c0 + 2 * hgk, "hg": c0 + 2 * hgk + hgv,
           "ma": c0 + 2 * hgk + 2 * hgv, "mb": c0 + 2 * hgk + 2 * hgv + d}
    gla_c_s = GLA_CHUNK if ts % GLA_CHUNK == 0 else ts
    hg_c_s = HG_CHUNK if ts % HG_CHUNK == 0 else ts
    gla_c_p = GLA_CHUNK if tp % GLA_CHUNK == 0 else tp
    hg_c_p = HG_CHUNK if tp % HG_CHUNK == 0 else tp

    x = jnp.concatenate([x_prompt.reshape(n_p, d), x_sample.reshape(n_s, d)], axis=0)
    xb = x.astype(BF16)
    new_gla_p, new_gla_s, new_hg_p, new_hg_s, new_conv_p, new_conv_s = [], [], [], [], [], []
    for l in range(depth):
        wl = w_in[l]
        w_main = jnp.concatenate([wl[:, :c0], wl[:, c0 + rank:]], axis=1).astype(BF16)
        w_lr = jnp.pad(wl[:, c0:c0 + rank], ((0, 0), (0, LANES - rank))).astype(BF16)
        wg2 = jnp.pad(w_gla_gate2[l], ((0, LANES - rank), (0, 0))).astype(BF16)
        p = _matmul(xb, w_main, tm_cap=1088, tn_cap=1024, name=f"in_proj_{l}")
        lr = _matmul(xb, w_lr, tm_cap=1088, tn_cap=LANES, name=f"in_proj_lr_{l}")

        gla_args = dict(layer=l, heads=heads, dk=dk, dv=dv, col=col)
        bg = b_gla_gate[l].reshape(1, qk)
        gn = g_gla_norm[l].reshape(1, dv)
        ya_p, sg_p = _gla_branch(p, lr, wg2, bg, gn, None, row0=0, nb=bp, t=tp, c=gla_c_p,
                                 name=f"gla_prompt_{l}", **gla_args)
        ya_s, sg_s = _gla_branch(p, lr, wg2, bg, gn, state_gla, row0=n_p, nb=bs, t=ts, c=gla_c_s,
                                 name=f"gla_sample_{l}", **gla_args)
        hg_args = dict(layer=l, heads=hheads, dk=hk, dv=hv, col=col)
        hgn = g_hgrn_norm[l].reshape(1, hv)
        m_p, sh_p = _hg_branch(p, ya_p, lower_bounds, hgn, None, row0=0, nb=bp, t=tp, c=hg_c_p,
                               name=f"hgrn_prompt_{l}", **hg_args)
        m_s, sh_s = _hg_branch(p, ya_s, lower_bounds, hgn, state_hgrn, row0=n_p, nb=bs, t=ts, c=hg_c_s,
                               name=f"hgrn_sample_{l}", **hg_args)
        merged = jnp.concatenate([m_p, m_s], axis=0)

        mix = _matmul(merged, w_out[l].astype(BF16), tm_cap=1088, tn_cap=1024, name=f"out_proj_{l}")
        x, xb = _residual_ln(x, mix, ln1_g[l], ln1_b[l], alpha, name=f"ln1_{l}")

        sc = state_ffn_conv[l]
        zero = jnp.zeros_like(sc[:, 0])
        h0 = jnp.stack([sc[:, 0], sc[:, 1]] + [zero] * (ts - 2), axis=1).reshape(n_s, f2)
        h1 = jnp.stack([sc[:, 1]] + [zero] * (ts - 1), axis=1).reshape(n_s, f2)
        (hb, tail_a, tail_g, us_a, us_g), tm = _up_conv_gate(
            xb, w_up[l].astype(BF16), conv_w[l], conv_b[l].reshape(1, f2), h0, h1,
            n_prompt=n_p, t_prompt=tp, ts=ts, name=f"up_conv_{l}")
        last_tiles = (jnp.arange(bp) + 1) * (tp // tm) - 1
        keep = slice(SUBLANES - (CONV_W - 1), SUBLANES)
        new_conv_p.append(jnp.concatenate([tail_a[last_tiles][:, keep], tail_g[last_tiles][:, keep]], axis=-1))
        new_conv_s.append(jnp.concatenate([us_a.reshape(bs, ts, f)[:, ts - (CONV_W - 1):],
                                           us_g.reshape(bs, ts, f)[:, ts - (CONV_W - 1):]], axis=-1))

        y = _matmul(hb, w_down[l].astype(BF16), tm_cap=512, tn_cap=512, name=f"down_proj_{l}")
        x, xb = _residual_ln(x, y, ln2_g[l], ln2_b[l], alpha, name=f"ln2_{l}")

        new_gla_p.append(sg_p)
        new_gla_s.append(sg_s)
        new_hg_p.append(sh_p)
        new_hg_s.append(sh_s)

    return (x[:n_p].reshape(bp, tp, d), x[n_p:].reshape(bs, ts, d),
            jnp.stack(new_gla_p), jnp.stack(new_gla_s), jnp.stack(new_hg_p), jnp.stack(new_hg_s),
            jnp.stack(new_conv_p), jnp.stack(new_conv_s))
```

```python
import functools

import jax
import jax.numpy as jnp
from jax import lax
from jax.experimental import pallas as pl
from jax.experimental.pallas import tpu as pltpu

GLA_CHUNK = 64
HG_CHUNK = 32
GLA_GATE_NORM = 16.0
EPS = 1e-5
CONV_W = 3

LANES = 128
SUBLANES = 8
BF16_ROWS = 16
V7X_VMEM_BYTES = 64 * 1024 * 1024
VMEM_LIMIT = V7X_VMEM_BYTES - 8 * 1024 * 1024

F32 = jnp.float32
BF16 = jnp.bfloat16
HIGHEST = lax.Precision.HIGHEST
_NT = (((1,), (1,)), ((), ()))
_TN = (((0,), (0,)), ((), ()))


def _divisor(n, cap, mult):
    if n <= cap:
        return n
    best = None
    for d in range(mult, cap + 1, mult):
        if n % d == 0:
            best = d
    assert best is not None, (n, cap, mult)
    return best


def _params(n_axes):
    return pltpu.CompilerParams(dimension_semantics=("arbitrary",) * n_axes, vmem_limit_bytes=VMEM_LIMIT)


def _sigmoid(x):
    return 1.0 / (1.0 + jnp.exp(-x))


def _silu(x):
    return x * _sigmoid(x)


def _log_sigmoid(x):
    return jnp.minimum(x, 0.0) - jnp.log1p(jnp.exp(-jnp.abs(x)))


def _mm_kernel(x_ref, w_ref, o_ref):
    o_ref[...] = jnp.dot(x_ref[...], w_ref[...], preferred_element_type=F32)


def _matmul(x, w, *, tm_cap, tn_cap, name):
    m, k = x.shape
    n = w.shape[1]
    tm = _divisor(m, tm_cap, BF16_ROWS)
    tn = _divisor(n, tn_cap, LANES)
    return pl.pallas_call(
        _mm_kernel,
        grid=(n // tn, m // tm),
        in_specs=[pl.BlockSpec((tm, k), lambda j, i: (i, 0)),
                  pl.BlockSpec((k, tn), lambda j, i: (0, j))],
        out_specs=pl.BlockSpec((tm, tn), lambda j, i: (i, j)),
        out_shape=jax.ShapeDtypeStruct((m, n), F32),
        compiler_params=_params(2),
        name=name,
    )(x, w)


def _ln_kernel(*refs, alpha, first_blocks, two_in, two_out):
    refs = list(refs)
    i = pl.program_id(0)
    in_first = i < first_blocks
    if two_in:
        xa_ref, xb_ref = refs.pop(0), refs.pop(0)
        x = jnp.where(in_first, xa_ref[...], xb_ref[...])
    else:
        x = refs.pop(0)[...]
    y_ref, g_ref, b_ref = refs[:3]
    z = alpha * x + y_ref[...]
    mu = jnp.mean(z, axis=-1, keepdims=True)
    zc = z - mu
    var = jnp.mean(zc * zc, axis=-1, keepdims=True)
    o = zc * lax.rsqrt(var + EPS) * g_ref[...] + b_ref[...]
    if two_out:
        oa_ref, ob_ref = refs[3:]

        @pl.when(in_first)
        def _():
            oa_ref[...] = o

        @pl.when(jnp.logical_not(in_first))
        def _():
            ob_ref[...] = o
    else:
        o32_ref, o16_ref = refs[3:]
        o32_ref[...] = o
        o16_ref[...] = o.astype(BF16)


def _residual_ln(x, y, g, b, alpha, *, n_first, split_out, name):
    n, d = y.shape
    tr = _divisor(_gcd(n_first, n - n_first), 256, BF16_ROWS)
    fb = n_first // tr
    row = pl.BlockSpec((tr, d), lambda i: (i, 0))
    first = pl.BlockSpec((tr, d), lambda i: (jnp.minimum(i, fb - 1), 0))
    second = pl.BlockSpec((tr, d), lambda i: (jnp.maximum(i - fb, 0), 0))
    vec = pl.BlockSpec((1, d), lambda i: (0, 0))
    two_in = isinstance(x, tuple)
    x_args, x_specs = (list(x), [first, second]) if two_in else ([x], [row])
    if split_out:
        out_specs = [first, second]
        out_shape = [jax.ShapeDtypeStruct((n_first, d), F32), jax.ShapeDtypeStruct((n - n_first, d), F32)]
    else:
        out_specs = [row, row]
        out_shape = [jax.ShapeDtypeStruct((n, d), F32), jax.ShapeDtypeStruct((n, d), BF16)]
    return pl.pallas_call(
        functools.partial(_ln_kernel, alpha=alpha, first_blocks=fb, two_in=two_in, two_out=split_out),
        grid=(n // tr,),
        in_specs=x_specs + [row, vec, vec],
        out_specs=out_specs,
        out_shape=out_shape,
        compiler_params=_params(1),
        name=name,
    )(*x_args, y, g.reshape(1, d), b.reshape(1, d))


def _chunk_prep(q, k, lg, c, groups):
    r = q.shape[0]
    shift = c.bit_length() - 1
    ri = lax.broadcasted_iota(jnp.int32, (r, r), 0)
    ci = lax.broadcasted_iota(jnp.int32, (r, r), 1)
    tri = ci <= ri
    grp = lax.broadcasted_iota(jnp.int32, (r, 1), 0) >> shift
    if groups > 1:
        tri = tri & ((ri >> shift) == (ci >> shift))
    b = jnp.dot(tri.astype(F32), lg, precision=HIGHEST, preferred_element_type=F32)
    b_ref = b[c // 2:c // 2 + 1]
    b_last = b[c - 1:c]
    for g in range(1, groups):
        b_ref = jnp.where(grp == g, b[g * c + c // 2:g * c + c // 2 + 1], b_ref)
        b_last = jnp.where(grp == g, b[g * c + c - 1:g * c + c], b_last)
    qs = (q * jnp.exp(b - b_ref)).astype(BF16)
    ks = (k * jnp.exp(b_ref - b)).astype(BF16)
    q_in = (q * jnp.exp(b)).astype(BF16)
    k_out = k * jnp.exp(b_last - b)
    return tri, grp, b, qs, ks, q_in, k_out


def _chunk_head(tri, grp, b, qs, ks, q_in, k_out, v, c, groups, s_in, s_out):
    dk = qs.shape[1]
    dv = v.shape[1]
    vb = v.astype(BF16)
    scores = lax.dot_general(qs, ks, _NT, preferred_element_type=F32)
    scores = jnp.where(tri, scores, 0.0)
    o = jnp.dot(scores.astype(BF16), vb, preferred_element_type=F32)
    ones = jnp.ones((SUBLANES, LANES), F32)
    row0 = lax.broadcasted_iota(jnp.int32, (SUBLANES, 1), 0) == 0
    for g in range(groups):
        og = jnp.dot(q_in, s_in(g)[...].astype(BF16), preferred_element_type=F32)
        o = o + (og if groups == 1 else jnp.where(grp == g, og, 0.0))
        kg = k_out if groups == 1 else jnp.where(grp == g, k_out, 0.0)
        upd = lax.dot_general(kg.astype(BF16), vb, _TN, preferred_element_type=F32)
        bl = jnp.where(row0, jnp.broadcast_to(b[g * c + c - 1:g * c + c], (SUBLANES, dk)), 0.0)
        dcol = jnp.exp(lax.dot_general(bl, ones, _TN, precision=HIGHEST, preferred_element_type=F32))
        for j in range(dv // LANES):
            sl = slice(j * LANES, (j + 1) * LANES)
            s_out(g)[:, sl] = s_in(g)[:, sl] * dcol + upd[:, sl]
    return o


def _rms(o, gain):
    return o * lax.rsqrt(jnp.mean(o * o, axis=-1, keepdims=True) + EPS) * gain


def _gla_kernel(q_ref, k_ref, v_ref, r_ref, ma_ref, lr_ref, wg_ref, bg_ref, gn_ref, *rest,
                c, groups, chunks, n_tblk, scale, has_state, n_aliased):
    rest = list(rest)
    if has_state:
        s0_ref = rest.pop(0)
    del rest[:n_aliased]
    if has_state:
        ya_ref, so_ref = rest
        s_in = lambda g: s0_ref.at[g, 0]
        s_out = lambda g: so_ref.at[g, 0]
    else:
        ya_ref, so_ref, s_scr = rest
        s_in = s_out = lambda g: s_scr.at[g]
        tb = pl.program_id(2)

        @pl.when(tb == 0)
        def _():
            s_scr[...] = jnp.zeros_like(s_scr)

    rows_per_chunk = groups * c

    def chunk(ci, carry):
        rows = pl.ds(pl.multiple_of(ci * rows_per_chunk, rows_per_chunk), rows_per_chunk)
        gate = jnp.dot(lr_ref[rows, :].astype(BF16), wg_ref[...], preferred_element_type=F32) + bg_ref[...]
        lg = _log_sigmoid(gate) * (1.0 / GLA_GATE_NORM)
        prep = _chunk_prep(q_ref[rows, :] * scale, k_ref[rows, :], lg, c, groups)
        o = _chunk_head(*prep, v_ref[rows, :], c, groups, s_in, s_out)
        y = _rms(o, gn_ref[...]) * _silu(r_ref[rows, :])
        ya_ref[rows, :] = _sigmoid(ma_ref[rows, :]) * y
        return carry

    lax.fori_loop(0, chunks, chunk, 0)

    if not has_state:
        @pl.when(tb == n_tblk - 1)
        def _():
            so_ref[0, 0] = s_scr[0]


def _gla_branch(p, lr, wg, bg, gn, state, s_all, *, depth, layer, row0, nb, t, c, heads, dk, dv, col, name):
    has_state = state is not None
    if has_state:
        assert t == c and SUBLANES % c == 0
        groups, chunks = SUBLANES // c, 1
    else:
        groups, chunks = 1, min(4, t // c)
    rblk = groups * c * chunks
    n_tblk = t // (c * chunks)
    assert nb % groups == 0 and row0 % rblk == 0 and t % (c * chunks) == 0
    rb0 = row0 // rblk

    def rowspec(width, off):
        assert off % width == 0
        return pl.BlockSpec((rblk, width), lambda b, h, tb: (rb0 + b * n_tblk + tb, off // width + h))

    in_specs = [rowspec(dk, col["q"]), rowspec(dk, col["k"]), rowspec(dv, col["v"]), rowspec(dv, col["r"]),
                rowspec(dv, col["ma"]),
                pl.BlockSpec((rblk, LANES), lambda b, h, tb: (rb0 + b * n_tblk + tb, 0)),
                pl.BlockSpec((LANES, dk), lambda b, h, tb: (0, h)),
                pl.BlockSpec((1, dk), lambda b, h, tb: (0, h)),
                pl.BlockSpec((1, dv), lambda b, h, tb: (0, 0))]
    args = [p, p, p, p, p, lr, wg, bg, gn]
    state_spec = pl.BlockSpec((None, groups, 1, dk, dv), lambda b, h, tb: (layer, b, h, 0, 0))
    scratch = []
    if has_state:
        in_specs.append(state_spec)
        args.append(state)
    else:
        scratch = [pltpu.VMEM((1, dk, dv), F32)]
    aliases = {}
    if s_all is not None:
        aliases[len(args)] = 1
        in_specs.append(pl.BlockSpec(memory_space=pl.ANY))
        args.append(s_all)
    ya, s_new = pl.pallas_call(
        functools.partial(_gla_kernel, c=c, groups=groups, chunks=chunks, n_tblk=n_tblk,
                          scale=dk ** -0.5, has_state=has_state, n_aliased=len(aliases)),
        grid=(nb // groups, heads, n_tblk),
        in_specs=in_specs,
        out_specs=[pl.BlockSpec((rblk, dv), lambda b, h, tb: (b * n_tblk + tb, h)), state_spec],
        out_shape=[jax.ShapeDtypeStruct((nb * t, heads * dv), F32),
                   jax.ShapeDtypeStruct((depth, nb, heads, dk, dv), F32)],
        input_output_aliases=aliases,
        scratch_shapes=scratch,
        compiler_params=_params(3),
        name=name,
    )(*args)
    return ya, s_new


def _hg_kernel(hq_ref, hf_ref, hi_ref, hg_ref, mb_ref, ya_ref, lb_ref, gn_ref, *rest,
               layer, c, groups, chunks, n_tblk, heads, dk, dv, scale, has_state, n_aliased):
    rest = list(rest)
    if has_state:
        s0_ref = rest.pop(0)
    del rest[:n_aliased]
    if has_state:
        out_ref, so_ref = rest
        s_in = lambda h: (lambda g: s0_ref.at[g, h])
        s_out = lambda h: (lambda g: so_ref.at[g, h])
    else:
        out_ref, so_ref, s_scr = rest
        s_in = s_out = lambda h: (lambda g: s_scr.at[h])
        tb = pl.program_id(1)

        @pl.when(tb == 0)
        def _():
            s_scr[...] = jnp.zeros_like(s_scr)

    lbw = lb_ref[...]
    e = jnp.exp(lbw - jnp.max(lbw, axis=0, keepdims=True))
    sm = e / jnp.sum(e, axis=0, keepdims=True)
    cum = sm[0:1]
    first = cum
    for i in range(1, layer + 1):
        cum = cum + sm[i:i + 1]
    lb = cum - first
    log_lb = jnp.log(lb)
    log_1m = jnp.log1p(-lb)
    one_m = 1.0 - lb

    rows_per_chunk = groups * c

    def chunk(ci, carry):
        rows = pl.ds(pl.multiple_of(ci * rows_per_chunk, rows_per_chunk), rows_per_chunk)
        zf = hf_ref[rows, :]
        x2 = log_1m + _log_sigmoid(zf)
        lg = jnp.maximum(log_lb, x2) + jnp.log1p(jnp.exp(-jnp.abs(log_lb - x2)))
        kk = one_m * _sigmoid(-zf)
        qq = _silu(hq_ref[rows, :]) * scale
        tri, grp, b, qs, ks, q_in, k_out = _chunk_prep(qq, kk, lg, c, groups)
        for h in range(heads):
            sk = slice(h * dk, (h + 1) * dk)
            sv = slice(h * dv, (h + 1) * dv)
            o = _chunk_head(tri, grp, b[:, sk], qs[:, sk], ks[:, sk], q_in[:, sk], k_out[:, sk],
                            hi_ref[rows, sv], c, groups, s_in(h), s_out(h))
            y = _rms(o, gn_ref[...]) * _silu(hg_ref[rows, sv])
            out_ref[rows, sv] = (ya_ref[rows, sv] + _sigmoid(mb_ref[rows, sv]) * y).astype(BF16)
        return carry

    lax.fori_loop(0, chunks, chunk, 0)

    if not has_state:
        @pl.when(tb == n_tblk - 1)
        def _():
            so_ref[0] = s_scr[...]


def _hg_branch(p, ya, lower_bounds, gn, state, merged_all, s_all, *, layer, row0, nb, t, c, heads, dk, dv, col,
               name):
    has_state = state is not None
    if has_state:
        assert t == c and BF16_ROWS % c == 0
        groups, chunks = BF16_ROWS // c, 1
    else:
        groups, chunks = 1, min(2, t // c)
    rblk = groups * c * chunks
    n_tblk = t // (c * chunks)
    width = heads * dk
    assert heads * dv == width and nb % groups == 0 and row0 % rblk == 0 and t % (c * chunks) == 0
    rb0 = row0 // rblk

    def rowspec(off):
        assert off % width == 0
        return pl.BlockSpec((rblk, width), lambda b, tb: (rb0 + b * n_tblk + tb, off // width))

    local = pl.BlockSpec((rblk, width), lambda b, tb: (b * n_tblk + tb, 0))
    depth = lower_bounds.shape[0]
    in_specs = [rowspec(col["hq"]), rowspec(col["hf"]), rowspec(col["hi"]), rowspec(col["hg"]),
                rowspec(col["mb"]), local,
                pl.BlockSpec((depth, width), lambda b, tb: (0, 0)),
                pl.BlockSpec((1, dv), lambda b, tb: (0, 0))]
    args = [p, p, p, p, p, ya, lower_bounds, gn]
    state_spec = pl.BlockSpec((None, groups, heads, dk, dv), lambda b, tb: (layer, b, 0, 0, 0))
    scratch = []
    if has_state:
        in_specs.append(state_spec)
        args.append(state)
    else:
        scratch = [pltpu.VMEM((heads, dk, dv), F32)]
    aliases = {}
    for out_idx, prior in enumerate((merged_all, s_all)):
        if prior is not None:
            aliases[len(args)] = out_idx
            in_specs.append(pl.BlockSpec(memory_space=pl.ANY))
            args.append(prior)
    merged, s_new = pl.pallas_call(
        functools.partial(_hg_kernel, layer=layer, c=c, groups=groups, chunks=chunks, n_tblk=n_tblk,
                          heads=heads, dk=dk, dv=dv, scale=dk ** -0.5, has_state=has_state,
                          n_aliased=len(aliases)),
        grid=(nb // groups, n_tblk),
        in_specs=in_specs,
        out_specs=[pl.BlockSpec((rblk, width), lambda b, tb: (rb0 + b * n_tblk + tb, 0)), state_spec],
        out_shape=[jax.ShapeDtypeStruct((p.shape[0], width), BF16),
                   jax.ShapeDtypeStruct((depth, nb, heads, dk, dv), F32)],
        input_output_aliases=aliases,
        scratch_shapes=scratch,
        compiler_params=_params(2),
        name=name,
    )(*args)
    return merged, s_new


def _up_kernel(x_ref, wa_ref, wg_ref, cwa_ref, cwg_ref, cba_ref, cbg_ref, h0a_ref, h0g_ref, h1a_ref, h1g_ref,
               h_ref, taila_ref, tailg_ref, usa_ref, usg_ref, sa_scr, sg_scr,
               *, tm, n_row_tiles, n_prompt_tiles, tiles_per_seq, ts):
    s = pl.program_id(0)
    hist = SUBLANES

    @pl.when(s == 0)
    def _():
        sa_scr[...] = jnp.zeros_like(sa_scr)
        sg_scr[...] = jnp.zeros_like(sg_scr)

    ip = jnp.maximum(s - 1, 0) % n_row_tiles
    is_sample = ip >= n_prompt_tiles
    seq_start = jnp.logical_and(jnp.logical_not(is_sample), (ip % tiles_per_seq) == 0)
    row = lax.broadcasted_iota(jnp.int32, (tm, 1), 0)
    pos = jnp.where(is_sample, row & (ts - 1), jnp.where(seq_start, row, CONV_W))
    keep1 = pos >= 1
    keep2 = pos >= 2

    def epilogue(cw_ref, cb_ref, h0_ref, h1_ref, tail_ref, us_ref, scr):
        u = scr[hist:hist + tm, :]
        u1 = jnp.where(keep1, scr[hist - 1:hist - 1 + tm, :], 0.0)
        u2 = jnp.where(keep2, scr[hist - 2:hist - 2 + tm, :], 0.0)
        u1 = u1 + jnp.where(is_sample, h1_ref[...], 0.0)
        u2 = u2 + jnp.where(is_sample, h0_ref[...], 0.0)
        uc = u2 * cw_ref[0:1, :] + u1 * cw_ref[1:2, :] + u * cw_ref[2:3, :] + cb_ref[...]
        tail = scr[tm:tm + hist, :]
        scr[0:hist, :] = tail
        tail_ref[0] = tail
        us_ref[...] = u
        return uc

    a = epilogue(cwa_ref, cba_ref, h0a_ref, h1a_ref, taila_ref, usa_ref, sa_scr)
    g = epilogue(cwg_ref, cbg_ref, h0g_ref, h1g_ref, tailg_ref, usg_ref, sg_scr)
    h_ref[...] = (_silu(a) * g).astype(BF16)

    x = x_ref[...]
    sa_scr[hist:hist + tm, :] = jnp.dot(x, wa_ref[...], preferred_element_type=F32)
    sg_scr[hist:hist + tm, :] = jnp.dot(x, wg_ref[...], preferred_element_type=F32)


def _up_conv_gate(xb, w, cw, cb, h0, h1, *, n_prompt, t_prompt, ts, name):
    n, d = xb.shape
    f = w.shape[1] // 2
    n_sample = n - n_prompt
    tm = _divisor(_gcd(t_prompt, n_sample), 512, BF16_ROWS)
    assert tm % ts == 0 and ts & (ts - 1) == 0 and ts >= CONV_W - 1 and tm >= SUBLANES
    tn = _divisor(f, 256, LANES)
    ncb = f // tn
    npt = n_prompt // tm
    nrt = n // tm
    n_tiles = ncb * nrt

    def cur(s):
        c = jnp.minimum(s, n_tiles - 1)
        return c // nrt, c % nrt

    def prev(s):
        c = jnp.maximum(s - 1, 0)
        return c // nrt, c % nrt

    def prev_sample(s, off):
        j, i = prev(s)
        return jnp.maximum(i - npt, 0), off + j

    in_specs = [pl.BlockSpec((tm, d), lambda s: (cur(s)[1], 0)),
                pl.BlockSpec((d, tn), lambda s: (0, cur(s)[0])),
                pl.BlockSpec((d, tn), lambda s: (0, ncb + cur(s)[0])),
                pl.BlockSpec((CONV_W, tn), lambda s: (0, prev(s)[0])),
                pl.BlockSpec((CONV_W, tn), lambda s: (0, ncb + prev(s)[0])),
                pl.BlockSpec((1, tn), lambda s: (0, prev(s)[0])),
                pl.BlockSpec((1, tn), lambda s: (0, ncb + prev(s)[0])),
                pl.BlockSpec((tm, tn), lambda s: prev_sample(s, 0)),
                pl.BlockSpec((tm, tn), lambda s: prev_sample(s, ncb)),
                pl.BlockSpec((tm, tn), lambda s: prev_sample(s, 0)),
                pl.BlockSpec((tm, tn), lambda s: prev_sample(s, ncb))]
    tail_spec = pl.BlockSpec((1, SUBLANES, tn), lambda s: (prev(s)[1], 0, prev(s)[0]))
    us_spec = pl.BlockSpec((tm, tn), lambda s: prev_sample(s, 0))
    out_specs = [pl.BlockSpec((tm, tn), lambda s: (prev(s)[1], prev(s)[0])), tail_spec, tail_spec,
                 us_spec, us_spec]
    out_shape = [jax.ShapeDtypeStruct((n, f), BF16),
                 jax.ShapeDtypeStruct((nrt, SUBLANES, f), F32), jax.ShapeDtypeStruct((nrt, SUBLANES, f), F32),
                 jax.ShapeDtypeStruct((n_sample, f), F32), jax.ShapeDtypeStruct((n_sample, f), F32)]
    return pl.pallas_call(
        functools.partial(_up_kernel, tm=tm, n_row_tiles=nrt, n_prompt_tiles=npt,
                          tiles_per_seq=t_prompt // tm, ts=ts),
        grid=(n_tiles + 1,),
        in_specs=in_specs,
        out_specs=out_specs,
        out_shape=out_shape,
        scratch_shapes=[pltpu.VMEM((tm + SUBLANES, tn), F32), pltpu.VMEM((tm + SUBLANES, tn), F32)],
        compiler_params=_params(1),
        name=name,
    )(xb, w, w, cw, cw, cb, cb, h0, h0, h1, h1), tm


def _gcd(a, b):
    while b:
        a, b = b, a % b
    return a


def kernel(x_prompt, x_sample, state_gla, state_hgrn, state_ffn_conv, w_in, w_gla_gate2, b_gla_gate, g_gla_norm,
           lower_bounds, g_hgrn_norm, w_out, ln1_g, ln1_b, w_up, conv_w, conv_b, w_down, ln2_g, ln2_b):
    bp, tp, d = x_prompt.shape
    bs, ts, _ = x_sample.shape
    depth, _, heads, dk, dv = state_gla.shape
    _, _, hheads, hk, hv = state_hgrn.shape
    rank = w_gla_gate2.shape[1]
    f2 = w_up.shape[2]
    f = f2 // 2
    qk, vw, hgk, hgv = heads * dk, heads * dv, hheads * hk, hheads * hv
    c0 = 2 * qk + 2 * vw
    assert conv_w.shape[1] == CONV_W and rank <= LANES
    n_p, n_s = bp * tp, bs * ts
    alpha = (2.0 * depth) ** 0.25
    col = {"q": 0, "k": qk, "v": 2 * qk, "r": 2 * qk + vw,
           "hq": c0, "hf": c0 + hgk, "hi": c0 + 2 * hgk, "hg": c0 + 2 * hgk + hgv,
           "ma": c0 + 2 * hgk + 2 * hgv, "mb": c0 + 2 * hgk + 2 * hgv + d}
    gla_c_s = GLA_CHUNK if ts % GLA_CHUNK == 0 else ts
    hg_c_s = HG_CHUNK if ts % HG_CHUNK == 0 else ts
    gla_c_p = GLA_CHUNK if tp % GLA_CHUNK == 0 else tp
    hg_c_p = HG_CHUNK if tp % HG_CHUNK == 0 else tp

    x = (x_prompt.reshape(n_p, d), x_sample.reshape(n_s, d))
    xb = jnp.concatenate([x[0].astype(BF16), x[1].astype(BF16)], axis=0)
    gla_p = gla_s = hgr_p = hgr_s = None
    new_conv_p, new_conv_s = [], []
    for l in range(depth):
        wl = w_in[l]
        w_main = jnp.concatenate([wl[:, :c0], wl[:, c0 + rank:]], axis=1).astype(BF16)
        w_lr = jnp.pad(wl[:, c0:c0 + rank], ((0, 0), (0, LANES - rank))).astype(BF16)
        wg2 = jnp.pad(w_gla_gate2[l], ((0, LANES - rank), (0, 0))).astype(BF16)
        p = _matmul(xb, w_main, tm_cap=1088, tn_cap=1024, name=f"in_proj_{l}")
        lr = _matmul(xb, w_lr, tm_cap=1088, tn_cap=LANES, name=f"in_proj_lr_{l}")

        gla_args = dict(depth=depth, layer=l, heads=heads, dk=dk, dv=dv, col=col)
        bg = b_gla_gate[l].reshape(1, qk)
        gn = g_gla_norm[l].reshape(1, dv)
        ya_p, gla_p = _gla_branch(p, lr, wg2, bg, gn, None, gla_p, row0=0, nb=bp, t=tp, c=gla_c_p,
                                  name=f"gla_prompt_{l}", **gla_args)
        ya_s, gla_s = _gla_branch(p, lr, wg2, bg, gn, state_gla, gla_s, row0=n_p, nb=bs, t=ts, c=gla_c_s,
                                  name=f"gla_sample_{l}", **gla_args)
        hg_args = dict(layer=l, heads=hheads, dk=hk, dv=hv, col=col)
        hgn = g_hgrn_norm[l].reshape(1, hv)
        merged, hgr_p = _hg_branch(p, ya_p, lower_bounds, hgn, None, None, hgr_p, row0=0, nb=bp, t=tp,
                                   c=hg_c_p, name=f"hgrn_prompt_{l}", **hg_args)
        merged, hgr_s = _hg_branch(p, ya_s, lower_bounds, hgn, state_hgrn, merged, hgr_s, row0=n_p, nb=bs,
                                   t=ts, c=hg_c_s, name=f"hgrn_sample_{l}", **hg_args)

        mix = _matmul(merged, w_out[l].astype(BF16), tm_cap=1088, tn_cap=1024, name=f"out_proj_{l}")
        x, xb = _residual_ln(x, mix, ln1_g[l], ln1_b[l], alpha, n_first=n_p, split_out=False, name=f"ln1_{l}")

        sc = state_ffn_conv[l]
        zero = jnp.zeros_like(sc[:, 0])
        h0 = jnp.stack([sc[:, 0], sc[:, 1]] + [zero] * (ts - 2), axis=1).reshape(n_s, f2)
        h1 = jnp.stack([sc[:, 1]] + [zero] * (ts - 1), axis=1).reshape(n_s, f2)
        (hb, tail_a, tail_g, us_a, us_g), tm = _up_conv_gate(
            xb, w_up[l].astype(BF16), conv_w[l], conv_b[l].reshape(1, f2), h0, h1,
            n_prompt=n_p, t_prompt=tp, ts=ts, name=f"up_conv_{l}")
        last_tiles = (jnp.arange(bp) + 1) * (tp // tm) - 1
        keep = slice(SUBLANES - (CONV_W - 1), SUBLANES)
        new_conv_p.append(jnp.concatenate([tail_a[last_tiles][:, keep], tail_g[last_tiles][:, keep]], axis=-1))
        new_conv_s.append(jnp.concatenate([us_a.reshape(bs, ts, f)[:, ts - (CONV_W - 1):],
                                           us_g.reshape(bs, ts, f)[:, ts - (CONV_W - 1):]], axis=-1))

        y = _matmul(hb, w_down[l].astype(BF16), tm_cap=512, tn_cap=512, name=f"down_proj_{l}")
        if l < depth - 1:
            x, xb = _residual_ln(x, y, ln2_g[l], ln2_b[l], alpha, n_first=n_p, split_out=False, name=f"ln2_{l}")
        else:
            y_prompt, y_sample = _residual_ln(x, y, ln2_g[l], ln2_b[l], alpha, n_first=n_p, split_out=True,
                                              name=f"ln2_{l}")

    return (y_prompt.reshape(bp, tp, d), y_sample.reshape(bs, ts, d), gla_p, gla_s, hgr_p, hgr_s,
            jnp.stack(new_conv_p), jnp.stack(new_conv_s))
```

```python
import functools

import jax
import jax.numpy as jnp
from jax import lax
from jax.experimental import pallas as pl
from jax.experimental.pallas import tpu as pltpu

GLA_CHUNK = 64
HG_CHUNK = 32
GLA_GATE_NORM = 16.0
EPS = 1e-5
CONV_W = 3

LANES = 128
SUBLANES = 8
BF16_ROWS = 16
V7X_VMEM_BYTES = 64 * 1024 * 1024
VMEM_LIMIT = V7X_VMEM_BYTES - 8 * 1024 * 1024

F32 = jnp.float32
BF16 = jnp.bfloat16
HIGHEST = lax.Precision.HIGHEST
_NT = (((1,), (1,)), ((), ()))
_TN = (((0,), (0,)), ((), ()))


def _divisor(n, cap, mult):
    if n <= cap:
        return n
    best = None
    for d in range(mult, cap + 1, mult):
        if n % d == 0:
            best = d
    assert best is not None, (n, cap, mult)
    return best


def _params(n_axes):
    return pltpu.CompilerParams(dimension_semantics=("arbitrary",) * n_axes, vmem_limit_bytes=VMEM_LIMIT)


def _sigmoid(x):
    return 1.0 / (1.0 + jnp.exp(-x))


def _silu(x):
    return x * _sigmoid(x)


def _log_sigmoid(x):
    return jnp.minimum(x, 0.0) - jnp.log1p(jnp.exp(-jnp.abs(x)))


def _mm_kernel(x_ref, w_ref, o_ref, *w_bf16):
    if w_bf16:
        (wb_ref,) = w_bf16

        @pl.when(pl.program_id(1) == 0)
        def _():
            wb_ref[...] = w_ref[...].astype(BF16)

        w = wb_ref[...]
    else:
        w = w_ref[...]
    o_ref[...] = jnp.dot(x_ref[...], w, preferred_element_type=F32)


def _matmul(x, w, layer, *, tm_cap, tn_cap, name):
    m, k = x.shape
    n = w.shape[2]
    tm = _divisor(m, tm_cap, BF16_ROWS)
    tn = _divisor(n, tn_cap, LANES)
    scratch = [pltpu.VMEM((k, tn), BF16)] if w.dtype == F32 else []
    return pl.pallas_call(
        _mm_kernel,
        grid=(n // tn, m // tm),
        in_specs=[pl.BlockSpec((tm, k), lambda j, i: (i, 0)),
                  pl.BlockSpec((None, k, tn), lambda j, i: (layer, 0, j))],
        out_specs=pl.BlockSpec((tm, tn), lambda j, i: (i, j)),
        out_shape=jax.ShapeDtypeStruct((m, n), F32),
        scratch_shapes=scratch,
        compiler_params=_params(2),
        name=name,
    )(x, w)


def _pack_kernel(*refs, shift):
    if shift:
        a_ref, b_ref, _, o_ref = refs
        both = jnp.concatenate([a_ref[...], b_ref[...]], axis=1)
        width = both.shape[1]
        o_ref[...] = pltpu.roll(both, width - shift, axis=1)[:, :o_ref.shape[1]].astype(BF16)
    else:
        a_ref, o_ref = refs
        o_ref[...] = a_ref[...].astype(BF16)


def _pack_in_weights(w_in, c0, rank):
    depth, k, n = w_in.shape
    n_hi = n - rank - c0
    assert c0 % LANES == 0 and n_hi % LANES == 0 and 0 < rank < LANES
    tn = _divisor(_gcd(c0, n_hi), 1024, LANES)
    tk = _divisor(k, 1024, SUBLANES)
    out_shape = jax.ShapeDtypeStruct((depth, k, c0 + n_hi), BF16)
    lo = pl.pallas_call(
        functools.partial(_pack_kernel, shift=0),
        grid=(depth, c0 // tn, k // tk),
        in_specs=[pl.BlockSpec((None, tk, tn), lambda l, j, kk: (l, kk, j))],
        out_specs=pl.BlockSpec((None, tk, tn), lambda l, j, kk: (l, kk, j)),
        out_shape=out_shape,
        compiler_params=_params(3),
        name="pack_w_in_lo",
    )(w_in)
    j0 = c0 // tn
    return pl.pallas_call(
        functools.partial(_pack_kernel, shift=rank),
        grid=(depth, n_hi // tn, k // tk),
        in_specs=[pl.BlockSpec((None, tk, tn), lambda l, j, kk: (l, kk, j0 + j)),
                  pl.BlockSpec((None, tk, LANES), lambda l, j, kk: (l, kk, (c0 + (j + 1) * tn) // LANES)),
                  pl.BlockSpec(memory_space=pl.ANY)],
        out_specs=pl.BlockSpec((None, tk, tn), lambda l, j, kk: (l, kk, j0 + j)),
        out_shape=out_shape,
        input_output_aliases={2: 0},
        compiler_params=_params(3),
        name="pack_w_in_hi",
    )(w_in, w_in, lo)


def _ln_kernel(*refs, alpha, first_blocks, two_in, two_out):
    refs = list(refs)
    i = pl.program_id(0)
    in_first = i < first_blocks
    if two_in:
        xa_ref, xb_ref = refs.pop(0), refs.pop(0)
        x = jnp.where(in_first, xa_ref[...], xb_ref[...])
    else:
        x = refs.pop(0)[...]
    y_ref, g_ref, b_ref = refs[:3]
    z = alpha * x + y_ref[...]
    mu = jnp.mean(z, axis=-1, keepdims=True)
    zc = z - mu
    var = jnp.mean(zc * zc, axis=-1, keepdims=True)
    o = zc * lax.rsqrt(var + EPS) * g_ref[...] + b_ref[...]
    if two_out:
        oa_ref, ob_ref = refs[3:]

        @pl.when(in_first)
        def _():
            oa_ref[...] = o

        @pl.when(jnp.logical_not(in_first))
        def _():
            ob_ref[...] = o
    else:
        o32_ref, o16_ref = refs[3:]
        o32_ref[...] = o
        o16_ref[...] = o.astype(BF16)


def _residual_ln(x, y, g, b, alpha, *, n_first, split_out, name):
    n, d = y.shape
    tr = _divisor(_gcd(n_first, n - n_first), 256, BF16_ROWS)
    fb = n_first // tr
    row = pl.BlockSpec((tr, d), lambda i: (i, 0))
    first = pl.BlockSpec((tr, d), lambda i: (jnp.minimum(i, fb - 1), 0))
    second = pl.BlockSpec((tr, d), lambda i: (jnp.maximum(i - fb, 0), 0))
    vec = pl.BlockSpec((1, d), lambda i: (0, 0))
    two_in = isinstance(x, tuple)
    x_args, x_specs = (list(x), [first, second]) if two_in else ([x], [row])
    if split_out:
        out_specs = [first, second]
        out_shape = [jax.ShapeDtypeStruct((n_first, d), F32), jax.ShapeDtypeStruct((n - n_first, d), F32)]
    else:
        out_specs = [row, row]
        out_shape = [jax.ShapeDtypeStruct((n, d), F32), jax.ShapeDtypeStruct((n, d), BF16)]
    return pl.pallas_call(
        functools.partial(_ln_kernel, alpha=alpha, first_blocks=fb, two_in=two_in, two_out=split_out),
        grid=(n // tr,),
        in_specs=x_specs + [row, vec, vec],
        out_specs=out_specs,
        out_shape=out_shape,
        compiler_params=_params(1),
        name=name,
    )(*x_args, y, g.reshape(1, d), b.reshape(1, d))


def _chunk_prep(q, k, lg, c, groups):
    r = q.shape[0]
    shift = c.bit_length() - 1
    ri = lax.broadcasted_iota(jnp.int32, (r, r), 0)
    ci = lax.broadcasted_iota(jnp.int32, (r, r), 1)
    tri = ci <= ri
    grp = lax.broadcasted_iota(jnp.int32, (r, 1), 0) >> shift
    if groups > 1:
        tri = tri & ((ri >> shift) == (ci >> shift))
    b = jnp.dot(tri.astype(F32), lg, precision=HIGHEST, preferred_element_type=F32)
    b_ref = b[c // 2:c // 2 + 1]
    b_last = b[c - 1:c]
    for g in range(1, groups):
        b_ref = jnp.where(grp == g, b[g * c + c // 2:g * c + c // 2 + 1], b_ref)
        b_last = jnp.where(grp == g, b[g * c + c - 1:g * c + c], b_last)
    qs = (q * jnp.exp(b - b_ref)).astype(BF16)
    ks = (k * jnp.exp(b_ref - b)).astype(BF16)
    q_in = (q * jnp.exp(b)).astype(BF16)
    k_out = k * jnp.exp(b_last - b)
    return tri, grp, b, qs, ks, q_in, k_out


def _chunk_head(tri, grp, b, qs, ks, q_in, k_out, v, c, groups, s_in, s_out, transposed=False):
    dk = qs.shape[1]
    dv = v.shape[1]
    vb = v.astype(BF16)
    scores = lax.dot_general(qs, ks, _NT, preferred_element_type=F32)
    scores = jnp.where(tri, scores, 0.0)
    o = jnp.dot(scores.astype(BF16), vb, preferred_element_type=F32)
    if transposed:
        assert groups == 1
        st_in, st_out = s_in(0), s_out(0)
        o = o + lax.dot_general(q_in, st_in[...].astype(BF16), _NT, preferred_element_type=F32)
        upd_t = lax.dot_general(vb, k_out.astype(BF16), _TN, preferred_element_type=F32)
        decay = jnp.exp(b[c - 1:c])
        for j in range(dv // LANES):
            sl = slice(j * LANES, (j + 1) * LANES)
            st_out[sl, :] = st_in[sl, :] * decay + upd_t[sl, :]
        return o
    ones = jnp.ones((SUBLANES, LANES), F32)
    row0 = lax.broadcasted_iota(jnp.int32, (SUBLANES, 1), 0) == 0
    for g in range(groups):
        og = jnp.dot(q_in, s_in(g)[...].astype(BF16), preferred_element_type=F32)
        o = o + (og if groups == 1 else jnp.where(grp == g, og, 0.0))
        kg = k_out if groups == 1 else jnp.where(grp == g, k_out, 0.0)
        upd = lax.dot_general(kg.astype(BF16), vb, _TN, preferred_element_type=F32)
        bl = jnp.where(row0, jnp.broadcast_to(b[g * c + c - 1:g * c + c], (SUBLANES, dk)), 0.0)
        dcol = jnp.exp(lax.dot_general(bl, ones, _TN, precision=HIGHEST, preferred_element_type=F32))
        for j in range(dv // LANES):
            sl = slice(j * LANES, (j + 1) * LANES)
            s_out(g)[:, sl] = s_in(g)[:, sl] * dcol + upd[:, sl]
    return o


def _rms(o, gain):
    return o * lax.rsqrt(jnp.mean(o * o, axis=-1, keepdims=True) + EPS) * gain


def _gla_kernel(q_ref, k_ref, v_ref, r_ref, ma_ref, lr_ref, wg_ref, bg_ref, gn_ref, *rest,
                c, groups, chunks, n_tblk, scale, has_state, n_aliased):
    rest = list(rest)
    if has_state:
        s0_ref = rest.pop(0)
    del rest[:n_aliased]
    if has_state:
        ya_ref, so_ref = rest
        s_in = lambda g: s0_ref.at[g, 0]
        s_out = lambda g: so_ref.at[g, 0]
    else:
        ya_ref, so_ref, s_scr = rest
        s_in = s_out = lambda g: s_scr.at[g]
        tb = pl.program_id(2)

        @pl.when(tb == 0)
        def _():
            s_scr[...] = jnp.zeros_like(s_scr)

    rows_per_chunk = groups * c

    def chunk(ci, carry):
        rows = pl.ds(pl.multiple_of(ci * rows_per_chunk, rows_per_chunk), rows_per_chunk)
        gate = jnp.dot(lr_ref[rows, :].astype(BF16), wg_ref[...], preferred_element_type=F32) + bg_ref[...]
        lg = _log_sigmoid(gate) * (1.0 / GLA_GATE_NORM)
        prep = _chunk_prep(q_ref[rows, :] * scale, k_ref[rows, :], lg, c, groups)
        o = _chunk_head(*prep, v_ref[rows, :], c, groups, s_in, s_out, transposed=not has_state)
        y = _rms(o, gn_ref[...]) * _silu(r_ref[rows, :])
        ya_ref[rows, :] = _sigmoid(ma_ref[rows, :]) * y
        return carry

    lax.fori_loop(0, chunks, chunk, 0)

    if not has_state:
        @pl.when(tb == n_tblk - 1)
        def _():
            so_ref[0, 0] = s_scr[0].T


def _gla_branch(p, lr, wg, bg, gn, state, s_all, *, depth, layer, row0, nb, t, c, heads, dk, dv, col, name):
    has_state = state is not None
    if has_state:
        assert t == c and SUBLANES % c == 0
        groups, chunks = SUBLANES // c, 1
    else:
        groups, chunks = 1, min(4, t // c)
    rblk = groups * c * chunks
    n_tblk = t // (c * chunks)
    assert nb % groups == 0 and row0 % rblk == 0 and t % (c * chunks) == 0
    rb0 = row0 // rblk

    def rowspec(width, off):
        assert off % width == 0
        return pl.BlockSpec((rblk, width), lambda b, h, tb: (rb0 + b * n_tblk + tb, off // width + h))

    in_specs = [rowspec(dk, col["q"]), rowspec(dk, col["k"]), rowspec(dv, col["v"]), rowspec(dv, col["r"]),
                rowspec(dv, col["ma"]),
                pl.BlockSpec((rblk, LANES), lambda b, h, tb: (rb0 + b * n_tblk + tb, 0)),
                pl.BlockSpec((LANES, dk), lambda b, h, tb: (0, h)),
                pl.BlockSpec((1, dk), lambda b, h, tb: (0, h)),
                pl.BlockSpec((1, dv), lambda b, h, tb: (0, 0))]
    args = [p, p, p, p, p, lr, wg, bg, gn]
    state_spec = pl.BlockSpec((None, groups, 1, dk, dv), lambda b, h, tb: (layer, b, h, 0, 0))
    scratch = []
    if has_state:
        in_specs.append(state_spec)
        args.append(state)
    else:
        scratch = [pltpu.VMEM((1, dv, dk), F32)]
    aliases = {}
    if s_all is not None:
        aliases[len(args)] = 1
        in_specs.append(pl.BlockSpec(memory_space=pl.ANY))
        args.append(s_all)
    ya, s_new = pl.pallas_call(
        functools.partial(_gla_kernel, c=c, groups=groups, chunks=chunks, n_tblk=n_tblk,
                          scale=dk ** -0.5, has_state=has_state, n_aliased=len(aliases)),
        grid=(nb // groups, heads, n_tblk),
        in_specs=in_specs,
        out_specs=[pl.BlockSpec((rblk, dv), lambda b, h, tb: (b * n_tblk + tb, h)), state_spec],
        out_shape=[jax.ShapeDtypeStruct((nb * t, heads * dv), F32),
                   jax.ShapeDtypeStruct((depth, nb, heads, dk, dv), F32)],
        input_output_aliases=aliases,
        scratch_shapes=scratch,
        compiler_params=_params(3),
        name=name,
    )(*args)
    return ya, s_new


def _hg_kernel(hq_ref, hf_ref, hi_ref, hg_ref, mb_ref, ya_ref, lb_ref, gn_ref, *rest,
               layer, c, groups, chunks, n_tblk, heads, dk, dv, scale, has_state, n_aliased):
    rest = list(rest)
    if has_state:
        s0_ref = rest.pop(0)
    del rest[:n_aliased]
    if has_state:
        out_ref, so_ref = rest
        s_in = lambda h: (lambda g: s0_ref.at[g, h])
        s_out = lambda h: (lambda g: so_ref.at[g, h])
    else:
        out_ref, so_ref, s_scr = rest
        s_in = s_out = lambda h: (lambda g: s_scr.at[h])
        tb = pl.program_id(1)

        @pl.when(tb == 0)
        def _():
            s_scr[...] = jnp.zeros_like(s_scr)

    lbw = lb_ref[...]
    e = jnp.exp(lbw - jnp.max(lbw, axis=0, keepdims=True))
    sm = e / jnp.sum(e, axis=0, keepdims=True)
    cum = sm[0:1]
    first = cum
    for i in range(1, layer + 1):
        cum = cum + sm[i:i + 1]
    lb = cum - first
    log_lb = jnp.log(lb)
    log_1m = jnp.log1p(-lb)
    one_m = 1.0 - lb

    rows_per_chunk = groups * c

    def chunk(ci, carry):
        rows = pl.ds(pl.multiple_of(ci * rows_per_chunk, rows_per_chunk), rows_per_chunk)
        zf = hf_ref[rows, :]
        ez = jnp.exp(-jnp.abs(zf))
        x2 = log_1m + (jnp.minimum(zf, 0.0) - jnp.log1p(ez))
        lg = jnp.maximum(log_lb, x2) + jnp.log1p(jnp.exp(-jnp.abs(log_lb - x2)))
        kk = one_m * (jnp.where(zf >= 0.0, ez, 1.0) / (1.0 + ez))
        qq = _silu(hq_ref[rows, :]) * scale
        tri, grp, b, qs, ks, q_in, k_out = _chunk_prep(qq, kk, lg, c, groups)
        for h in range(heads):
            sk = slice(h * dk, (h + 1) * dk)
            sv = slice(h * dv, (h + 1) * dv)
            o = _chunk_head(tri, grp, b[:, sk], qs[:, sk], ks[:, sk], q_in[:, sk], k_out[:, sk],
                            hi_ref[rows, sv], c, groups, s_in(h), s_out(h), transposed=not has_state)
            y = _rms(o, gn_ref[...]) * _silu(hg_ref[rows, sv])
            out_ref[rows, sv] = (ya_ref[rows, sv] + _sigmoid(mb_ref[rows, sv]) * y).astype(BF16)
        return carry

    lax.fori_loop(0, chunks, chunk, 0)

    if not has_state:
        @pl.when(tb == n_tblk - 1)
        def _():
            for h in range(heads):
                so_ref[0, h] = s_scr[h].T


def _hg_branch(p, ya, lower_bounds, gn, state, merged_all, s_all, *, layer, row0, nb, t, c, heads, dk, dv, col,
               name):
    has_state = state is not None
    if has_state:
        assert t == c and BF16_ROWS % c == 0
        groups, chunks = BF16_ROWS // c, 1
    else:
        groups, chunks = 1, min(2, t // c)
    rblk = groups * c * chunks
    n_tblk = t // (c * chunks)
    width = heads * dk
    assert heads * dv == width and nb % groups == 0 and row0 % rblk == 0 and t % (c * chunks) == 0
    rb0 = row0 // rblk

    def rowspec(off):
        assert off % width == 0
        return pl.BlockSpec((rblk, width), lambda b, tb: (rb0 + b * n_tblk + tb, off // width))

    local = pl.BlockSpec((rblk, width), lambda b, tb: (b * n_tblk + tb, 0))
    depth = lower_bounds.shape[0]
    in_specs = [rowspec(col["hq"]), rowspec(col["hf"]), rowspec(col["hi"]), rowspec(col["hg"]),
                rowspec(col["mb"]), local,
                pl.BlockSpec((depth, width), lambda b, tb: (0, 0)),
                pl.BlockSpec((1, dv), lambda b, tb: (0, 0))]
    args = [p, p, p, p, p, ya, lower_bounds, gn]
    state_spec = pl.BlockSpec((None, groups, heads, dk, dv), lambda b, tb: (layer, b, 0, 0, 0))
    scratch = []
    if has_state:
        in_specs.append(state_spec)
        args.append(state)
    else:
        scratch = [pltpu.VMEM((heads, dv, dk), F32)]
    aliases = {}
    for out_idx, prior in enumerate((merged_all, s_all)):
        if prior is not None:
            aliases[len(args)] = out_idx
            in_specs.append(pl.BlockSpec(memory_space=pl.ANY))
            args.append(prior)
    merged, s_new = pl.pallas_call(
        functools.partial(_hg_kernel, layer=layer, c=c, groups=groups, chunks=chunks, n_tblk=n_tblk,
                          heads=heads, dk=dk, dv=dv, scale=dk ** -0.5, has_state=has_state,
                          n_aliased=len(aliases)),
        grid=(nb // groups, n_tblk),
        in_specs=in_specs,
        out_specs=[pl.BlockSpec((rblk, width), lambda b, tb: (rb0 + b * n_tblk + tb, 0)), state_spec],
        out_shape=[jax.ShapeDtypeStruct((p.shape[0], width), BF16),
                   jax.ShapeDtypeStruct((depth, nb, heads, dk, dv), F32)],
        input_output_aliases=aliases,
        scratch_shapes=scratch,
        compiler_params=_params(2),
        name=name,
    )(*args)
    return merged, s_new


def _up_kernel(x_ref, wa_ref, wg_ref, cwa_ref, cwg_ref, cba_ref, cbg_ref, h0a_ref, h0g_ref, h1a_ref, h1g_ref,
               h_ref, taila_ref, tailg_ref, usa_ref, usg_ref, sa_scr, sg_scr, wa_bf16, wg_bf16,
               *, tm, n_tiles, n_row_tiles, n_prompt_tiles, tiles_per_seq, ts):
    s = pl.program_id(0)
    hist = SUBLANES

    @pl.when(s == 0)
    def _():
        sa_scr[...] = jnp.zeros_like(sa_scr)
        sg_scr[...] = jnp.zeros_like(sg_scr)

    @pl.when(jnp.logical_and(s % n_row_tiles == 0, s < n_tiles))
    def _():
        wa_bf16[...] = wa_ref[...].astype(BF16)
        wg_bf16[...] = wg_ref[...].astype(BF16)

    ip = jnp.maximum(s - 1, 0) % n_row_tiles
    is_sample = ip >= n_prompt_tiles
    seq_start = jnp.logical_and(jnp.logical_not(is_sample), (ip % tiles_per_seq) == 0)
    row = lax.broadcasted_iota(jnp.int32, (tm, 1), 0)
    pos = jnp.where(is_sample, row & (ts - 1), jnp.where(seq_start, row, CONV_W))
    keep1 = pos >= 1
    keep2 = pos >= 2

    def epilogue(cw_ref, cb_ref, h0_ref, h1_ref, tail_ref, us_ref, scr):
        u = scr[hist:hist + tm, :]
        u1 = jnp.where(keep1, scr[hist - 1:hist - 1 + tm, :], 0.0)
        u2 = jnp.where(keep2, scr[hist - 2:hist - 2 + tm, :], 0.0)
        u1 = u1 + jnp.where(is_sample, h1_ref[...], 0.0)
        u2 = u2 + jnp.where(is_sample, h0_ref[...], 0.0)
        uc = u2 * cw_ref[0:1, :] + u1 * cw_ref[1:2, :] + u * cw_ref[2:3, :] + cb_ref[...]
        tail = scr[tm:tm + hist, :]
        scr[0:hist, :] = tail
        tail_ref[0] = tail
        us_ref[...] = u
        return uc

    a = epilogue(cwa_ref, cba_ref, h0a_ref, h1a_ref, taila_ref, usa_ref, sa_scr)
    g = epilogue(cwg_ref, cbg_ref, h0g_ref, h1g_ref, tailg_ref, usg_ref, sg_scr)
    h_ref[...] = (_silu(a) * g).astype(BF16)

    x = x_ref[...]
    sa_scr[hist:hist + tm, :] = jnp.dot(x, wa_bf16[...], preferred_element_type=F32)
    sg_scr[hist:hist + tm, :] = jnp.dot(x, wg_bf16[...], preferred_element_type=F32)


def _up_conv_gate(xb, w, layer, cw, cb, h0, h1, *, n_prompt, t_prompt, ts, name):
    n, d = xb.shape
    f = w.shape[2] // 2
    n_sample = n - n_prompt
    tm = _divisor(_gcd(t_prompt, n_sample), 512, BF16_ROWS)
    assert tm % ts == 0 and ts & (ts - 1) == 0 and ts >= CONV_W - 1 and tm >= SUBLANES
    tn = _divisor(f, 256, LANES)
    ncb = f // tn
    npt = n_prompt // tm
    nrt = n // tm
    n_tiles = ncb * nrt

    def cur(s):
        c = jnp.minimum(s, n_tiles - 1)
        return c // nrt, c % nrt

    def prev(s):
        c = jnp.maximum(s - 1, 0)
        return c // nrt, c % nrt

    def prev_sample(s, off):
        j, i = prev(s)
        return jnp.maximum(i - npt, 0), off + j

    in_specs = [pl.BlockSpec((tm, d), lambda s: (cur(s)[1], 0)),
                pl.BlockSpec((None, d, tn), lambda s: (layer, 0, cur(s)[0])),
                pl.BlockSpec((None, d, tn), lambda s: (layer, 0, ncb + cur(s)[0])),
                pl.BlockSpec((CONV_W, tn), lambda s: (0, prev(s)[0])),
                pl.BlockSpec((CONV_W, tn), lambda s: (0, ncb + prev(s)[0])),
                pl.BlockSpec((1, tn), lambda s: (0, prev(s)[0])),
                pl.BlockSpec((1, tn), lambda s: (0, ncb + prev(s)[0])),
                pl.BlockSpec((tm, tn), lambda s: prev_sample(s, 0)),
                pl.BlockSpec((tm, tn), lambda s: prev_sample(s, ncb)),
                pl.BlockSpec((tm, tn), lambda s: prev_sample(s, 0)),
                pl.BlockSpec((tm, tn), lambda s: prev_sample(s, ncb))]
    tail_spec = pl.BlockSpec((1, SUBLANES, tn), lambda s: (prev(s)[1], 0, prev(s)[0]))
    us_spec = pl.BlockSpec((tm, tn), lambda s: prev_sample(s, 0))
    out_specs = [pl.BlockSpec((tm, tn), lambda s: (prev(s)[1], prev(s)[0])), tail_spec, tail_spec,
                 us_spec, us_spec]
    out_shape = [jax.ShapeDtypeStruct((n, f), BF16),
                 jax.ShapeDtypeStruct((nrt, SUBLANES, f), F32), jax.ShapeDtypeStruct((nrt, SUBLANES, f), F32),
                 jax.ShapeDtypeStruct((n_sample, f), F32), jax.ShapeDtypeStruct((n_sample, f), F32)]
    return pl.pallas_call(
        functools.partial(_up_kernel, tm=tm, n_tiles=n_tiles, n_row_tiles=nrt, n_prompt_tiles=npt,
                          tiles_per_seq=t_prompt // tm, ts=ts),
        grid=(n_tiles + 1,),
        in_specs=in_specs,
        out_specs=out_specs,
        out_shape=out_shape,
        scratch_shapes=[pltpu.VMEM((tm + SUBLANES, tn), F32), pltpu.VMEM((tm + SUBLANES, tn), F32),
                        pltpu.VMEM((d, tn), BF16), pltpu.VMEM((d, tn), BF16)],
        compiler_params=_params(1),
        name=name,
    )(xb, w, w, cw, cw, cb, cb, h0, h0, h1, h1), tm


def _gcd(a, b):
    while b:
        a, b = b, a % b
    return a


def kernel(x_prompt, x_sample, state_gla, state_hgrn, state_ffn_conv, w_in, w_gla_gate2, b_gla_gate, g_gla_norm,
           lower_bounds, g_hgrn_norm, w_out, ln1_g, ln1_b, w_up, conv_w, conv_b, w_down, ln2_g, ln2_b):
    bp, tp, d = x_prompt.shape
    bs, ts, _ = x_sample.shape
    depth, _, heads, dk, dv = state_gla.shape
    _, _, hheads, hk, hv = state_hgrn.shape
    rank = w_gla_gate2.shape[1]
    f2 = w_up.shape[2]
    f = f2 // 2
    qk, vw, hgk, hgv = heads * dk, heads * dv, hheads * hk, hheads * hv
    c0 = 2 * qk + 2 * vw
    assert conv_w.shape[1] == CONV_W and rank <= LANES
    n_p, n_s = bp * tp, bs * ts
    alpha = (2.0 * depth) ** 0.25
    col = {"q": 0, "k": qk, "v": 2 * qk, "r": 2 * qk + vw,
           "hq": c0, "hf": c0 + hgk, "hi": c0 + 2 * hgk, "hg": c0 + 2 * hgk + hgv,
           "ma": c0 + 2 * hgk + 2 * hgv, "mb": c0 + 2 * hgk + 2 * hgv + d}
    gla_c_s = GLA_CHUNK if ts % GLA_CHUNK == 0 else ts
    hg_c_s = HG_CHUNK if ts % HG_CHUNK == 0 else ts
    gla_c_p = GLA_CHUNK if tp % GLA_CHUNK == 0 else tp
    hg_c_p = HG_CHUNK if tp % HG_CHUNK == 0 else tp

    x = (x_prompt.reshape(n_p, d), x_sample.reshape(n_s, d))
    xb = jnp.concatenate([x[0].astype(BF16), x[1].astype(BF16)], axis=0)
    gla_p = gla_s = hgr_p = hgr_s = None
    new_conv_p, new_conv_s = [], []
    w_main = _pack_in_weights(w_in, c0, rank)
    w_lr = jnp.pad(w_in[:, :, c0:c0 + rank], ((0, 0), (0, 0), (0, LANES - rank))).astype(BF16)
    w_down_bf16 = w_down.astype(BF16)
    for l in range(depth):
        wg2 = jnp.pad(w_gla_gate2[l], ((0, LANES - rank), (0, 0))).astype(BF16)
        p = _matmul(xb, w_main, l, tm_cap=1088, tn_cap=1024, name=f"in_proj_{l}")
        lr = _matmul(xb, w_lr, l, tm_cap=1088, tn_cap=LANES, name=f"in_proj_lr_{l}")

        gla_args = dict(depth=depth, layer=l, heads=heads, dk=dk, dv=dv, col=col)
        bg = b_gla_gate[l].reshape(1, qk)
        gn = g_gla_norm[l].reshape(1, dv)
        ya_p, gla_p = _gla_branch(p, lr, wg2, bg, gn, None, gla_p, row0=0, nb=bp, t=tp, c=gla_c_p,
                                  name=f"gla_prompt_{l}", **gla_args)
        ya_s, gla_s = _gla_branch(p, lr, wg2, bg, gn, state_gla, gla_s, row0=n_p, nb=bs, t=ts, c=gla_c_s,
                                  name=f"gla_sample_{l}", **gla_args)
        hg_args = dict(layer=l, heads=hheads, dk=hk, dv=hv, col=col)
        hgn = g_hgrn_norm[l].reshape(1, hv)
        merged, hgr_p = _hg_branch(p, ya_p, lower_bounds, hgn, None, None, hgr_p, row0=0, nb=bp, t=tp,
                                   c=hg_c_p, name=f"hgrn_prompt_{l}", **hg_args)
        merged, hgr_s = _hg_branch(p, ya_s, lower_bounds, hgn, state_hgrn, merged, hgr_s, row0=n_p, nb=bs,
                                   t=ts, c=hg_c_s, name=f"hgrn_sample_{l}", **hg_args)

        mix = _matmul(merged, w_out, l, tm_cap=1088, tn_cap=512, name=f"out_proj_{l}")
        x, xb = _residual_ln(x, mix, ln1_g[l], ln1_b[l], alpha, n_first=n_p, split_out=False, name=f"ln1_{l}")

        sc = state_ffn_conv[l]
        zero = jnp.zeros_like(sc[:, 0])
        h0 = jnp.stack([sc[:, 0], sc[:, 1]] + [zero] * (ts - 2), axis=1).reshape(n_s, f2)
        h1 = jnp.stack([sc[:, 1]] + [zero] * (ts - 1), axis=1).reshape(n_s, f2)
        (hb, tail_a, tail_g, us_a, us_g), tm = _up_conv_gate(
            xb, w_up, l, conv_w[l], conv_b[l].reshape(1, f2), h0, h1,
            n_prompt=n_p, t_prompt=tp, ts=ts, name=f"up_conv_{l}")
        last_tiles = (jnp.arange(bp) + 1) * (tp // tm) - 1
        keep = slice(SUBLANES - (CONV_W - 1), SUBLANES)
        new_conv_p.append(jnp.concatenate([tail_a[last_tiles][:, keep], tail_g[last_tiles][:, keep]], axis=-1))
        new_conv_s.append(jnp.concatenate([us_a.reshape(bs, ts, f)[:, ts - (CONV_W - 1):],
                                           us_g.reshape(bs, ts, f)[:, ts - (CONV_W - 1):]], axis=-1))

        y = _matmul(hb, w_down_bf16, l, tm_cap=512, tn_cap=512, name=f"down_proj_{l}")
        if l < depth - 1:
            x, xb = _residual_ln(x, y, ln2_g[l], ln2_b[l], alpha, n_first=n_p, split_out=False, name=f"ln2_{l}")
        else:
            y_prompt, y_sample = _residual_ln(x, y, ln2_g[l], ln2_b[l], alpha, n_first=n_p, split_out=True,
                                              name=f"ln2_{l}")

    return (y_prompt.reshape(bp, tp, d), y_sample.reshape(bs, ts, d), gla_p, gla_s, hgr_p, hgr_s,
            jnp.stack(new_conv_p), jnp.stack(new_conv_s))
```

```python
import functools

import jax
import jax.numpy as jnp
from jax import lax
from jax.experimental import pallas as pl
from jax.experimental.pallas import tpu as pltpu

GLA_CHUNK = 64
HG_CHUNK = 32
GLA_GATE_NORM = 16.0
EPS = 1e-5
CONV_W = 3

LANES = 128
SUBLANES = 8
BF16_ROWS = 16
V7X_VMEM_BYTES = 64 * 1024 * 1024
VMEM_LIMIT = V7X_VMEM_BYTES - 8 * 1024 * 1024

F32 = jnp.float32
BF16 = jnp.bfloat16
HIGHEST = lax.Precision.HIGHEST
_NT = (((1,), (1,)), ((), ()))
_TN = (((0,), (0,)), ((), ()))


def _divisor(n, cap, mult):
    if n <= cap:
        return n
    best = None
    for d in range(mult, cap + 1, mult):
        if n % d == 0:
            best = d
    assert best is not None, (n, cap, mult)
    return best


def _params(n_axes):
    return pltpu.CompilerParams(dimension_semantics=("arbitrary",) * n_axes, vmem_limit_bytes=VMEM_LIMIT)


def _sigmoid(x):
    return 1.0 / (1.0 + jnp.exp(-x))


def _silu(x):
    return x * _sigmoid(x)


def _log_sigmoid(x):
    return jnp.minimum(x, 0.0) - jnp.log1p(jnp.exp(-jnp.abs(x)))


def _mm_kernel(x_ref, w_ref, o_ref, *w_bf16, w_is_nk):
    if w_bf16:
        (wb_ref,) = w_bf16

        @pl.when(pl.program_id(1) == 0)
        def _():
            wb_ref[...] = w_ref[...].astype(BF16)

        w = wb_ref[...]
    else:
        w = w_ref[...]
    if w_is_nk:
        o_ref[...] = lax.dot_general(x_ref[...], w, _NT, preferred_element_type=F32)
    else:
        o_ref[...] = jnp.dot(x_ref[...], w, preferred_element_type=F32)


def _matmul(x, w, layer, *, tm_cap, tn_cap, name, w_is_nk=False):
    m, k = x.shape
    n = w.shape[1] if w_is_nk else w.shape[2]
    tm = _divisor(m, tm_cap, BF16_ROWS)
    tn = _divisor(n, tn_cap, LANES)
    w_block = (None, tn, k) if w_is_nk else (None, k, tn)
    w_index = (lambda j, i: (layer, j, 0)) if w_is_nk else (lambda j, i: (layer, 0, j))
    scratch = [pltpu.VMEM(w_block[1:], BF16)] if w.dtype == F32 else []
    return pl.pallas_call(
        functools.partial(_mm_kernel, w_is_nk=w_is_nk),
        grid=(n // tn, m // tm),
        in_specs=[pl.BlockSpec((tm, k), lambda j, i: (i, 0)), pl.BlockSpec(w_block, w_index)],
        out_specs=pl.BlockSpec((tm, tn), lambda j, i: (i, j)),
        out_shape=jax.ShapeDtypeStruct((m, n), F32),
        scratch_shapes=scratch,
        compiler_params=_params(2),
        name=name,
    )(x, w)


def _pack_kernel(*refs, shift):
    if shift:
        a_ref, b_ref, _, o_ref = refs
        keep = o_ref.shape[0] - shift
        o_ref[0:keep, :] = a_ref[shift:, :].astype(BF16)
        o_ref[keep:, :] = b_ref[...].astype(BF16)
    else:
        a_ref, o_ref = refs
        o_ref[...] = a_ref[...].astype(BF16)


def _gate_rows_kernel(w_ref, o_ref):
    rank = w_ref.shape[0]
    o_ref[0:rank, :] = w_ref[...].astype(BF16)
    o_ref[rank:, :] = jnp.zeros((o_ref.shape[0] - rank, o_ref.shape[1]), BF16)


def _pack_in_weights(w_in, c0, rank):
    depth, k, n = w_in.shape
    wt = jnp.transpose(w_in, (0, 2, 1))
    n_hi = n - rank - c0
    assert c0 % LANES == 0 and n_hi % LANES == 0 and rank % BF16_ROWS == 0 and rank < LANES
    tr = _divisor(_gcd(c0, n_hi), 256, LANES)
    out_shape = jax.ShapeDtypeStruct((depth, c0 + n_hi, k), BF16)
    lo = pl.pallas_call(
        functools.partial(_pack_kernel, shift=0),
        grid=(depth, c0 // tr),
        in_specs=[pl.BlockSpec((None, tr, k), lambda l, j: (l, j, 0))],
        out_specs=pl.BlockSpec((None, tr, k), lambda l, j: (l, j, 0)),
        out_shape=out_shape,
        compiler_params=_params(2),
        name="pack_w_in_lo",
    )(wt)
    j0 = c0 // tr
    main = pl.pallas_call(
        functools.partial(_pack_kernel, shift=rank),
        grid=(depth, n_hi // tr),
        in_specs=[pl.BlockSpec((None, tr, k), lambda l, j: (l, j0 + j, 0)),
                  pl.BlockSpec((None, rank, k), lambda l, j: (l, (c0 + (j + 1) * tr) // rank, 0)),
                  pl.BlockSpec(memory_space=pl.ANY)],
        out_specs=pl.BlockSpec((None, tr, k), lambda l, j: (l, j0 + j, 0)),
        out_shape=out_shape,
        input_output_aliases={2: 0},
        compiler_params=_params(2),
        name="pack_w_in_hi",
    )(wt, wt, lo)
    gate = pl.pallas_call(
        _gate_rows_kernel,
        grid=(depth,),
        in_specs=[pl.BlockSpec((None, rank, k), lambda l: (l, c0 // rank, 0))],
        out_specs=pl.BlockSpec((None, LANES, k), lambda l: (l, 0, 0)),
        out_shape=jax.ShapeDtypeStruct((depth, LANES, k), BF16),
        compiler_params=_params(1),
        name="pack_w_in_gate",
    )(wt)
    return main, gate


def _ln_kernel(*refs, alpha, first_blocks, two_in, two_out):
    refs = list(refs)
    i = pl.program_id(0)
    in_first = i < first_blocks
    if two_in:
        xa_ref, xb_ref = refs.pop(0), refs.pop(0)
        x = jnp.where(in_first, xa_ref[...], xb_ref[...])
    else:
        x = refs.pop(0)[...]
    y_ref, g_ref, b_ref = refs[:3]
    z = alpha * x + y_ref[...]
    mu = jnp.mean(z, axis=-1, keepdims=True)
    zc = z - mu
    var = jnp.mean(zc * zc, axis=-1, keepdims=True)
    o = zc * lax.rsqrt(var + EPS) * g_ref[...] + b_ref[...]
    if two_out:
        oa_ref, ob_ref = refs[3:]

        @pl.when(in_first)
        def _():
            oa_ref[...] = o

        @pl.when(jnp.logical_not(in_first))
        def _():
            ob_ref[...] = o
    else:
        o32_ref, o16_ref = refs[3:]
        o32_ref[...] = o
        o16_ref[...] = o.astype(BF16)


def _residual_ln(x, y, g, b, alpha, *, n_first, split_out, name):
    n, d = y.shape
    tr = _divisor(_gcd(n_first, n - n_first), 256, BF16_ROWS)
    fb = n_first // tr
    row = pl.BlockSpec((tr, d), lambda i: (i, 0))
    first = pl.BlockSpec((tr, d), lambda i: (jnp.minimum(i, fb - 1), 0))
    second = pl.BlockSpec((tr, d), lambda i: (jnp.maximum(i - fb, 0), 0))
    vec = pl.BlockSpec((1, d), lambda i: (0, 0))
    two_in = isinstance(x, tuple)
    x_args, x_specs = (list(x), [first, second]) if two_in else ([x], [row])
    if split_out:
        out_specs = [first, second]
        out_shape = [jax.ShapeDtypeStruct((n_first, d), F32), jax.ShapeDtypeStruct((n - n_first, d), F32)]
    else:
        out_specs = [row, row]
        out_shape = [jax.ShapeDtypeStruct((n, d), F32), jax.ShapeDtypeStruct((n, d), BF16)]
    return pl.pallas_call(
        functools.partial(_ln_kernel, alpha=alpha, first_blocks=fb, two_in=two_in, two_out=split_out),
        grid=(n // tr,),
        in_specs=x_specs + [row, vec, vec],
        out_specs=out_specs,
        out_shape=out_shape,
        compiler_params=_params(1),
        name=name,
    )(*x_args, y, g.reshape(1, d), b.reshape(1, d))


def _chunk_prep(q, k, lg, c, groups):
    r = q.shape[0]
    shift = c.bit_length() - 1
    ri = lax.broadcasted_iota(jnp.int32, (r, r), 0)
    ci = lax.broadcasted_iota(jnp.int32, (r, r), 1)
    tri = ci <= ri
    grp = lax.broadcasted_iota(jnp.int32, (r, 1), 0) >> shift
    if groups > 1:
        tri = tri & ((ri >> shift) == (ci >> shift))
    b = jnp.dot(tri.astype(F32), lg, precision=HIGHEST, preferred_element_type=F32)
    b_ref = b[c // 2:c // 2 + 1]
    b_last = b[c - 1:c]
    for g in range(1, groups):
        b_ref = jnp.where(grp == g, b[g * c + c // 2:g * c + c // 2 + 1], b_ref)
        b_last = jnp.where(grp == g, b[g * c + c - 1:g * c + c], b_last)
    qs = (q * jnp.exp(b - b_ref)).astype(BF16)
    ks = (k * jnp.exp(b_ref - b)).astype(BF16)
    q_in = (q * jnp.exp(b)).astype(BF16)
    k_out = k * jnp.exp(b_last - b)
    return tri, grp, b, qs, ks, q_in, k_out


def _chunk_head(tri, grp, b, qs, ks, q_in, k_out, v, c, groups, s_in, s_out, decay_cols=None):
    transposed = decay_cols is None
    dk = qs.shape[1]
    dv = v.shape[1]
    vb = v.astype(BF16)
    scores = lax.dot_general(qs, ks, _NT, preferred_element_type=F32)
    scores = jnp.where(tri, scores, 0.0)
    o = jnp.dot(scores.astype(BF16), vb, preferred_element_type=F32)
    if transposed:
        assert groups == 1
        st_in, st_out = s_in(0), s_out(0)
        o = o + lax.dot_general(q_in, st_in[...].astype(BF16), _NT, preferred_element_type=F32)
        upd_t = lax.dot_general(vb, k_out.astype(BF16), _TN, preferred_element_type=F32)
        decay = jnp.exp(b[c - 1:c])
        for j in range(dv // LANES):
            sl = slice(j * LANES, (j + 1) * LANES)
            st_out[sl, :] = st_in[sl, :] * decay + upd_t[sl, :]
        return o
    for g in range(groups):
        og = jnp.dot(q_in, s_in(g)[...].astype(BF16), preferred_element_type=F32)
        o = o + (og if groups == 1 else jnp.where(grp == g, og, 0.0))
        kg = k_out if groups == 1 else jnp.where(grp == g, k_out, 0.0)
        upd = lax.dot_general(kg.astype(BF16), vb, _TN, preferred_element_type=F32)
        for j in range(dv // LANES):
            sl = slice(j * LANES, (j + 1) * LANES)
            s_out(g)[:, sl] = s_in(g)[:, sl] * decay_cols[g] + upd[:, sl]
    return o


def _decay_columns(b, c, groups):
    width = b.shape[1]
    ones = jnp.ones((SUBLANES, LANES), F32)
    row0 = lax.broadcasted_iota(jnp.int32, (SUBLANES, 1), 0) == 0
    cols = []
    for g in range(groups):
        bl = jnp.where(row0, jnp.broadcast_to(b[g * c + c - 1:g * c + c], (SUBLANES, width)), 0.0)
        cols.append(jnp.exp(lax.dot_general(bl, ones, _TN, precision=HIGHEST, preferred_element_type=F32)))
    return cols


def _rms(o, gain):
    return o * lax.rsqrt(jnp.mean(o * o, axis=-1, keepdims=True) + EPS) * gain


def _gla_kernel(q_ref, k_ref, v_ref, r_ref, ma_ref, lr_ref, wg_ref, bg_ref, gn_ref, *rest,
                c, groups, chunks, n_tblk, scale, has_state, n_aliased):
    rest = list(rest)
    if has_state:
        s0_ref = rest.pop(0)
    del rest[:n_aliased]
    if has_state:
        ya_ref, so_ref = rest
        s_in = lambda g: s0_ref.at[g, 0]
        s_out = lambda g: so_ref.at[g, 0]
    else:
        ya_ref, so_ref, s_scr = rest
        s_in = s_out = lambda g: s_scr.at[g]
        tb = pl.program_id(2)

        @pl.when(tb == 0)
        def _():
            s_scr[...] = jnp.zeros_like(s_scr)

    rows_per_chunk = groups * c

    def chunk(ci, carry):
        rows = pl.ds(pl.multiple_of(ci * rows_per_chunk, rows_per_chunk), rows_per_chunk)
        gate = jnp.dot(lr_ref[rows, :].astype(BF16), wg_ref[...], preferred_element_type=F32) + bg_ref[...]
        lg = _log_sigmoid(gate) * (1.0 / GLA_GATE_NORM)
        prep = _chunk_prep(q_ref[rows, :] * scale, k_ref[rows, :], lg, c, groups)
        cols = _decay_columns(prep[2], c, groups) if has_state else None
        o = _chunk_head(*prep, v_ref[rows, :], c, groups, s_in, s_out, cols)
        y = _rms(o, gn_ref[...]) * _silu(r_ref[rows, :])
        ya_ref[rows, :] = _sigmoid(ma_ref[rows, :]) * y
        return carry

    lax.fori_loop(0, chunks, chunk, 0)

    if not has_state:
        @pl.when(tb == n_tblk - 1)
        def _():
            so_ref[0, 0] = s_scr[0].T


def _gla_branch(p, lr, wg, bg, gn, state, s_all, *, depth, layer, row0, nb, t, c, heads, dk, dv, col, name):
    has_state = state is not None
    if has_state:
        assert t == c and SUBLANES % c == 0
        groups, chunks = SUBLANES // c, 1
    else:
        groups, chunks = 1, min(4, t // c)
    rblk = groups * c * chunks
    n_tblk = t // (c * chunks)
    assert nb % groups == 0 and row0 % rblk == 0 and t % (c * chunks) == 0
    rb0 = row0 // rblk

    def rowspec(width, off):
        assert off % width == 0
        return pl.BlockSpec((rblk, width), lambda b, h, tb: (rb0 + b * n_tblk + tb, off // width + h))

    in_specs = [rowspec(dk, col["q"]), rowspec(dk, col["k"]), rowspec(dv, col["v"]), rowspec(dv, col["r"]),
                rowspec(dv, col["ma"]),
                pl.BlockSpec((rblk, LANES), lambda b, h, tb: (rb0 + b * n_tblk + tb, 0)),
                pl.BlockSpec((LANES, dk), lambda b, h, tb: (0, h)),
                pl.BlockSpec((1, dk), lambda b, h, tb: (0, h)),
                pl.BlockSpec((1, dv), lambda b, h, tb: (0, 0))]
    args = [p, p, p, p, p, lr, wg, bg, gn]
    state_spec = pl.BlockSpec((None, groups, 1, dk, dv), lambda b, h, tb: (layer, b, h, 0, 0))
    scratch = []
    if has_state:
        in_specs.append(state_spec)
        args.append(state)
    else:
        scratch = [pltpu.VMEM((1, dv, dk), F32)]
    aliases = {}
    if s_all is not None:
        aliases[len(args)] = 1
        in_specs.append(pl.BlockSpec(memory_space=pl.ANY))
        args.append(s_all)
    ya, s_new = pl.pallas_call(
        functools.partial(_gla_kernel, c=c, groups=groups, chunks=chunks, n_tblk=n_tblk,
                          scale=dk ** -0.5, has_state=has_state, n_aliased=len(aliases)),
        grid=(nb // groups, heads, n_tblk),
        in_specs=in_specs,
        out_specs=[pl.BlockSpec((rblk, dv), lambda b, h, tb: (b * n_tblk + tb, h)), state_spec],
        out_shape=[jax.ShapeDtypeStruct((nb * t, heads * dv), F32),
                   jax.ShapeDtypeStruct((depth, nb, heads, dk, dv), F32)],
        input_output_aliases=aliases,
        scratch_shapes=scratch,
        compiler_params=_params(3),
        name=name,
    )(*args)
    return ya, s_new


def _hg_kernel(hq_ref, hf_ref, hi_ref, hg_ref, mb_ref, ya_ref, lb_ref, gn_ref, *rest,
               layer, c, groups, chunks, n_tblk, heads, dk, dv, scale, has_state, n_aliased):
    rest = list(rest)
    if has_state:
        s0_ref = rest.pop(0)
    del rest[:n_aliased]
    if has_state:
        out_ref, so_ref = rest
        s_in = lambda h: (lambda g: s0_ref.at[g, h])
        s_out = lambda h: (lambda g: so_ref.at[g, h])
    else:
        out_ref, so_ref, s_scr = rest
        s_in = s_out = lambda h: (lambda g: s_scr.at[h])
        tb = pl.program_id(1)

        @pl.when(tb == 0)
        def _():
            s_scr[...] = jnp.zeros_like(s_scr)

    lbw = lb_ref[...]
    e = jnp.exp(lbw - jnp.max(lbw, axis=0, keepdims=True))
    sm = e / jnp.sum(e, axis=0, keepdims=True)
    cum = sm[0:1]
    first = cum
    for i in range(1, layer + 1):
        cum = cum + sm[i:i + 1]
    lb = cum - first
    log_lb = jnp.log(lb)
    log_1m = jnp.log1p(-lb)
    one_m = 1.0 - lb

    rows_per_chunk = groups * c

    def chunk(ci, carry):
        rows = pl.ds(pl.multiple_of(ci * rows_per_chunk, rows_per_chunk), rows_per_chunk)
        zf = hf_ref[rows, :]
        ez = jnp.exp(-jnp.abs(zf))
        x2 = log_1m + (jnp.minimum(zf, 0.0) - jnp.log1p(ez))
        lg = jnp.maximum(log_lb, x2) + jnp.log1p(jnp.exp(-jnp.abs(log_lb - x2)))
        kk = one_m * (jnp.where(zf >= 0.0, ez, 1.0) / (1.0 + ez))
        qq = _silu(hq_ref[rows, :]) * scale
        tri, grp, b, qs, ks, q_in, k_out = _chunk_prep(qq, kk, lg, c, groups)
        hb = min(heads, 8)
        for h in range(heads):
            sk = slice(h * dk, (h + 1) * dk)
            sv = slice(h * dv, (h + 1) * dv)
            if has_state and h % hb == 0:
                cols = _decay_columns(b[:, h * dk:(h + hb) * dk], c, groups)
            head_cols = [col[(h % hb) * dk:(h % hb + 1) * dk, :] for col in cols] if has_state else None
            o = _chunk_head(tri, grp, b[:, sk], qs[:, sk], ks[:, sk], q_in[:, sk], k_out[:, sk],
                            hi_ref[rows, sv], c, groups, s_in(h), s_out(h), head_cols)
            y = _rms(o, gn_ref[...]) * _silu(hg_ref[rows, sv])
            out_ref[rows, sv] = (ya_ref[rows, sv] + _sigmoid(mb_ref[rows, sv]) * y).astype(BF16)
        return carry

    lax.fori_loop(0, chunks, chunk, 0)

    if not has_state:
        @pl.when(tb == n_tblk - 1)
        def _():
            for h in range(heads):
                so_ref[0, h] = s_scr[h].T


def _hg_branch(p, ya, lower_bounds, gn, state, merged_all, s_all, *, layer, row0, nb, t, c, heads, dk, dv, col,
               name):
    has_state = state is not None
    if has_state:
        assert t == c and BF16_ROWS % c == 0
        groups, chunks = BF16_ROWS // c, 1
    else:
        groups, chunks = 1, min(2, t // c)
    rblk = groups * c * chunks
    n_tblk = t // (c * chunks)
    width = heads * dk
    assert heads * dv == width and nb % groups == 0 and row0 % rblk == 0 and t % (c * chunks) == 0
    rb0 = row0 // rblk

    def rowspec(off):
        assert off % width == 0
        return pl.BlockSpec((rblk, width), lambda b, tb: (rb0 + b * n_tblk + tb, off // width))

    local = pl.BlockSpec((rblk, width), lambda b, tb: (b * n_tblk + tb, 0))
    depth = lower_bounds.shape[0]
    in_specs = [rowspec(col["hq"]), rowspec(col["hf"]), rowspec(col["hi"]), rowspec(col["hg"]),
                rowspec(col["mb"]), local,
                pl.BlockSpec((depth, width), lambda b, tb: (0, 0)),
                pl.BlockSpec((1, dv), lambda b, tb: (0, 0))]
    args = [p, p, p, p, p, ya, lower_bounds, gn]
    state_spec = pl.BlockSpec((None, groups, heads, dk, dv), lambda b, tb: (layer, b, 0, 0, 0))
    scratch = []
    if has_state:
        in_specs.append(state_spec)
        args.append(state)
    else:
        scratch = [pltpu.VMEM((heads, dv, dk), F32)]
    aliases = {}
    for out_idx, prior in enumerate((merged_all, s_all)):
        if prior is not None:
            aliases[len(args)] = out_idx
            in_specs.append(pl.BlockSpec(memory_space=pl.ANY))
            args.append(prior)
    merged, s_new = pl.pallas_call(
        functools.partial(_hg_kernel, layer=layer, c=c, groups=groups, chunks=chunks, n_tblk=n_tblk,
                          heads=heads, dk=dk, dv=dv, scale=dk ** -0.5, has_state=has_state,
                          n_aliased=len(aliases)),
        grid=(nb // groups, n_tblk),
        in_specs=in_specs,
        out_specs=[pl.BlockSpec((rblk, width), lambda b, tb: (rb0 + b * n_tblk + tb, 0)), state_spec],
        out_shape=[jax.ShapeDtypeStruct((p.shape[0], width), BF16),
                   jax.ShapeDtypeStruct((depth, nb, heads, dk, dv), F32)],
        input_output_aliases=aliases,
        scratch_shapes=scratch,
        compiler_params=_params(2),
        name=name,
    )(*args)
    return merged, s_new


def _up_kernel(x_ref, wa_ref, wg_ref, cwa_ref, cwg_ref, cba_ref, cbg_ref, h0a_ref, h0g_ref, h1a_ref, h1g_ref,
               h_ref, taila_ref, tailg_ref, usa_ref, usg_ref, sa_scr, sg_scr, wa_bf16, wg_bf16,
               *, tm, n_tiles, n_row_tiles, n_prompt_tiles, tiles_per_seq, ts):
    s = pl.program_id(0)
    hist = SUBLANES

    @pl.when(s == 0)
    def _():
        sa_scr[...] = jnp.zeros_like(sa_scr)
        sg_scr[...] = jnp.zeros_like(sg_scr)

    @pl.when(jnp.logical_and(s % n_row_tiles == 0, s < n_tiles))
    def _():
        wa_bf16[...] = wa_ref[...].astype(BF16)
        wg_bf16[...] = wg_ref[...].astype(BF16)

    ip = jnp.maximum(s - 1, 0) % n_row_tiles
    is_sample = ip >= n_prompt_tiles
    seq_start = jnp.logical_and(jnp.logical_not(is_sample), (ip % tiles_per_seq) == 0)
    row = lax.broadcasted_iota(jnp.int32, (tm, 1), 0)
    pos = jnp.where(is_sample, row & (ts - 1), jnp.where(seq_start, row, CONV_W))
    keep1 = pos >= 1
    keep2 = pos >= 2

    def epilogue(cw_ref, cb_ref, h0_ref, h1_ref, tail_ref, us_ref, scr):
        u = scr[hist:hist + tm, :]
        u1 = jnp.where(keep1, scr[hist - 1:hist - 1 + tm, :], 0.0)
        u2 = jnp.where(keep2, scr[hist - 2:hist - 2 + tm, :], 0.0)
        u1 = u1 + jnp.where(is_sample, h1_ref[...], 0.0)
        u2 = u2 + jnp.where(is_sample, h0_ref[...], 0.0)
        uc = u2 * cw_ref[0:1, :] + u1 * cw_ref[1:2, :] + u * cw_ref[2:3, :] + cb_ref[...]
        tail = scr[tm:tm + hist, :]
        scr[0:hist, :] = tail
        tail_ref[0] = tail
        us_ref[...] = u
        return uc

    a = epilogue(cwa_ref, cba_ref, h0a_ref, h1a_ref, taila_ref, usa_ref, sa_scr)
    g = epilogue(cwg_ref, cbg_ref, h0g_ref, h1g_ref, tailg_ref, usg_ref, sg_scr)
    h_ref[...] = (_silu(a) * g).astype(BF16)

    x = x_ref[...]
    sa_scr[hist:hist + tm, :] = jnp.dot(x, wa_bf16[...], preferred_element_type=F32)
    sg_scr[hist:hist + tm, :] = jnp.dot(x, wg_bf16[...], preferred_element_type=F32)


def _up_conv_gate(xb, w, layer, cw, cb, h0, h1, *, n_prompt, t_prompt, ts, name):
    n, d = xb.shape
    f = w.shape[2] // 2
    n_sample = n - n_prompt
    tm = _divisor(_gcd(t_prompt, n_sample), 512, BF16_ROWS)
    assert tm % ts == 0 and ts & (ts - 1) == 0 and ts >= CONV_W - 1 and tm >= SUBLANES
    tn = _divisor(f, 256, LANES)
    ncb = f // tn
    npt = n_prompt // tm
    nrt = n // tm
    n_tiles = ncb * nrt

    def cur(s):
        c = jnp.minimum(s, n_tiles - 1)
        return c // nrt, c % nrt

    def prev(s):
        c = jnp.maximum(s - 1, 0)
        return c // nrt, c % nrt

    def prev_sample(s, off):
        j, i = prev(s)
        return jnp.maximum(i - npt, 0), off + j

    in_specs = [pl.BlockSpec((tm, d), lambda s: (cur(s)[1], 0)),
                pl.BlockSpec((None, d, tn), lambda s: (layer, 0, cur(s)[0])),
                pl.BlockSpec((None, d, tn), lambda s: (layer, 0, ncb + cur(s)[0])),
                pl.BlockSpec((CONV_W, tn), lambda s: (0, prev(s)[0])),
                pl.BlockSpec((CONV_W, tn), lambda s: (0, ncb + prev(s)[0])),
                pl.BlockSpec((1, tn), lambda s: (0, prev(s)[0])),
                pl.BlockSpec((1, tn), lambda s: (0, ncb + prev(s)[0])),
                pl.BlockSpec((tm, tn), lambda s: prev_sample(s, 0)),
                pl.BlockSpec((tm, tn), lambda s: prev_sample(s, ncb)),
                pl.BlockSpec((tm, tn), lambda s: prev_sample(s, 0)),
                pl.BlockSpec((tm, tn), lambda s: prev_sample(s, ncb))]
    tail_spec = pl.BlockSpec((1, SUBLANES, tn), lambda s: (prev(s)[1], 0, prev(s)[0]))
    us_spec = pl.BlockSpec((tm, tn), lambda s: prev_sample(s, 0))
    out_specs = [pl.BlockSpec((tm, tn), lambda s: (prev(s)[1], prev(s)[0])), tail_spec, tail_spec,
                 us_spec, us_spec]
    out_shape = [jax.ShapeDtypeStruct((n, f), BF16),
                 jax.ShapeDtypeStruct((nrt, SUBLANES, f), F32), jax.ShapeDtypeStruct((nrt, SUBLANES, f), F32),
                 jax.ShapeDtypeStruct((n_sample, f), F32), jax.ShapeDtypeStruct((n_sample, f), F32)]
    return pl.pallas_call(
        functools.partial(_up_kernel, tm=tm, n_tiles=n_tiles, n_row_tiles=nrt, n_prompt_tiles=npt,
                          tiles_per_seq=t_prompt // tm, ts=ts),
        grid=(n_tiles + 1,),
        in_specs=in_specs,
        out_specs=out_specs,
        out_shape=out_shape,
        scratch_shapes=[pltpu.VMEM((tm + SUBLANES, tn), F32), pltpu.VMEM((tm + SUBLANES, tn), F32),
                        pltpu.VMEM((d, tn), BF16), pltpu.VMEM((d, tn), BF16)],
        compiler_params=_params(1),
        name=name,
    )(xb, w, w, cw, cw, cb, cb, h0, h0, h1, h1), tm


def _gcd(a, b):
    while b:
        a, b = b, a % b
    return a


def kernel(x_prompt, x_sample, state_gla, state_hgrn, state_ffn_conv, w_in, w_gla_gate2, b_gla_gate, g_gla_norm,
           lower_bounds, g_hgrn_norm, w_out, ln1_g, ln1_b, w_up, conv_w, conv_b, w_down, ln2_g, ln2_b):
    bp, tp, d = x_prompt.shape
    bs, ts, _ = x_sample.shape
    depth, _, heads, dk, dv = state_gla.shape
    _, _, hheads, hk, hv = state_hgrn.shape
    rank = w_gla_gate2.shape[1]
    f2 = w_up.shape[2]
    f = f2 // 2
    qk, vw, hgk, hgv = heads * dk, heads * dv, hheads * hk, hheads * hv
    c0 = 2 * qk + 2 * vw
    assert conv_w.shape[1] == CONV_W and rank <= LANES
    n_p, n_s = bp * tp, bs * ts
    alpha = (2.0 * depth) ** 0.25
    col = {"q": 0, "k": qk, "v": 2 * qk, "r": 2 * qk + vw,
           "hq": c0, "hf": c0 + hgk, "hi": c0 + 2 * hgk, "hg": c0 + 2 * hgk + hgv,
           "ma": c0 + 2 * hgk + 2 * hgv, "mb": c0 + 2 * hgk + 2 * hgv + d}
    gla_c_s = GLA_CHUNK if ts % GLA_CHUNK == 0 else ts
    hg_c_s = HG_CHUNK if ts % HG_CHUNK == 0 else ts
    gla_c_p = GLA_CHUNK if tp % GLA_CHUNK == 0 else tp
    hg_c_p = HG_CHUNK if tp % HG_CHUNK == 0 else tp

    x = (x_prompt.reshape(n_p, d), x_sample.reshape(n_s, d))
    xb = jnp.concatenate([x[0].astype(BF16), x[1].astype(BF16)], axis=0)
    gla_p = gla_s = hgr_p = hgr_s = None
    new_conv_p, new_conv_s = [], []
    w_main, w_lr = _pack_in_weights(w_in, c0, rank)
    w_down_bf16 = w_down.astype(BF16)
    for l in range(depth):
        wg2 = jnp.pad(w_gla_gate2[l], ((0, LANES - rank), (0, 0))).astype(BF16)
        p = _matmul(xb, w_main, l, tm_cap=1088, tn_cap=1024, w_is_nk=True, name=f"in_proj_{l}")
        lr = _matmul(xb, w_lr, l, tm_cap=1088, tn_cap=LANES, w_is_nk=True, name=f"in_proj_lr_{l}")

        gla_args = dict(depth=depth, layer=l, heads=heads, dk=dk, dv=dv, col=col)
        bg = b_gla_gate[l].reshape(1, qk)
        gn = g_gla_norm[l].reshape(1, dv)
        ya_p, gla_p = _gla_branch(p, lr, wg2, bg, gn, None, gla_p, row0=0, nb=bp, t=tp, c=gla_c_p,
                                  name=f"gla_prompt_{l}", **gla_args)
        ya_s, gla_s = _gla_branch(p, lr, wg2, bg, gn, state_gla, gla_s, row0=n_p, nb=bs, t=ts, c=gla_c_s,
                                  name=f"gla_sample_{l}", **gla_args)
        hg_args = dict(layer=l, heads=hheads, dk=hk, dv=hv, col=col)
        hgn = g_hgrn_norm[l].reshape(1, hv)
        merged, hgr_p = _hg_branch(p, ya_p, lower_bounds, hgn, None, None, hgr_p, row0=0, nb=bp, t=tp,
                                   c=hg_c_p, name=f"hgrn_prompt_{l}", **hg_args)
        merged, hgr_s = _hg_branch(p, ya_s, lower_bounds, hgn, state_hgrn, merged, hgr_s, row0=n_p, nb=bs,
                                   t=ts, c=hg_c_s, name=f"hgrn_sample_{l}", **hg_args)

        mix = _matmul(merged, w_out, l, tm_cap=1088, tn_cap=512, name=f"out_proj_{l}")
        x, xb = _residual_ln(x, mix, ln1_g[l], ln1_b[l], alpha, n_first=n_p, split_out=False, name=f"ln1_{l}")

        sc = state_ffn_conv[l]
        zero = jnp.zeros_like(sc[:, 0])
        h0 = jnp.stack([sc[:, 0], sc[:, 1]] + [zero] * (ts - 2), axis=1).reshape(n_s, f2)
        h1 = jnp.stack([sc[:, 1]] + [zero] * (ts - 1), axis=1).reshape(n_s, f2)
        (hb, tail_a, tail_g, us_a, us_g), tm = _up_conv_gate(
            xb, w_up, l, conv_w[l], conv_b[l].reshape(1, f2), h0, h1,
            n_prompt=n_p, t_prompt=tp, ts=ts, name=f"up_conv_{l}")
        last_tiles = (jnp.arange(bp) + 1) * (tp // tm) - 1
        keep = slice(SUBLANES - (CONV_W - 1), SUBLANES)
        new_conv_p.append(jnp.concatenate([tail_a[last_tiles][:, keep], tail_g[last_tiles][:, keep]], axis=-1))
        new_conv_s.append(jnp.concatenate([us_a.reshape(bs, ts, f)[:, ts - (CONV_W - 1):],
                                           us_g.reshape(bs, ts, f)[:, ts - (CONV_W - 1):]], axis=-1))

        y = _matmul(hb, w_down_bf16, l, tm_cap=512, tn_cap=512, name=f"down_proj_{l}")
        if l < depth - 1:
            x, xb = _residual_ln(x, y, ln2_g[l], ln2_b[l], alpha, n_first=n_p, split_out=False, name=f"ln2_{l}")
        else:
            y_prompt, y_sample = _residual_ln(x, y, ln2_g[l], ln2_b[l], alpha, n_first=n_p, split_out=True,
                                              name=f"ln2_{l}")

    return (y_prompt.reshape(bp, tp, d), y_sample.reshape(bs, ts, d), gla_p, gla_s, hgr_p, hgr_s,
            jnp.stack(new_conv_p), jnp.stack(new_conv_s))
```

```python
import functools

import jax
import jax.numpy as jnp
from jax import lax
from jax.experimental import pallas as pl
from jax.experimental.pallas import tpu as pltpu

GLA_CHUNK = 64
HG_CHUNK = 32
GLA_GATE_NORM = 16.0
EPS = 1e-5
CONV_W = 3

LANES = 128
SUBLANES = 8
BF16_ROWS = 16
V7X_VMEM_BYTES = 64 * 1024 * 1024
VMEM_LIMIT = V7X_VMEM_BYTES - 8 * 1024 * 1024

F32 = jnp.float32
BF16 = jnp.bfloat16
HIGHEST = lax.Precision.HIGHEST
_NT = (((1,), (1,)), ((), ()))
_TN = (((0,), (0,)), ((), ()))


def _divisor(n, cap, mult):
    if n <= cap:
        return n
    best = None
    for d in range(mult, cap + 1, mult):
        if n % d == 0:
            best = d
    assert best is not None, (n, cap, mult)
    return best


def _params(n_axes):
    return pltpu.CompilerParams(dimension_semantics=("arbitrary",) * n_axes, vmem_limit_bytes=VMEM_LIMIT)


def _sigmoid(x):
    return 1.0 / (1.0 + jnp.exp(-x))


def _silu(x):
    return x * _sigmoid(x)


def _log_sigmoid(x):
    return jnp.minimum(x, 0.0) - jnp.log1p(jnp.exp(-jnp.abs(x)))


def _mm_kernel(x_ref, w_ref, o_ref, *w_bf16, w_is_nk):
    if w_bf16:
        (wb_ref,) = w_bf16

        @pl.when(pl.program_id(1) == 0)
        def _():
            wb_ref[...] = w_ref[...].astype(BF16)

        w = wb_ref[...]
    else:
        w = w_ref[...]
    if w_is_nk:
        o_ref[...] = lax.dot_general(x_ref[...], w, _NT, preferred_element_type=F32)
    else:
        o_ref[...] = jnp.dot(x_ref[...], w, preferred_element_type=F32)


def _matmul(x, w, layer, *, tm_cap, tn_cap, name, w_is_nk=False):
    m, k = x.shape
    n = w.shape[1] if w_is_nk else w.shape[2]
    tm = _divisor(m, tm_cap, BF16_ROWS)
    tn = _divisor(n, tn_cap, LANES)
    w_block = (None, tn, k) if w_is_nk else (None, k, tn)
    w_index = (lambda j, i: (layer, j, 0)) if w_is_nk else (lambda j, i: (layer, 0, j))
    scratch = [pltpu.VMEM(w_block[1:], BF16)] if w.dtype == F32 else []
    return pl.pallas_call(
        functools.partial(_mm_kernel, w_is_nk=w_is_nk),
        grid=(n // tn, m // tm),
        in_specs=[pl.BlockSpec((tm, k), lambda j, i: (i, 0)), pl.BlockSpec(w_block, w_index)],
        out_specs=pl.BlockSpec((tm, tn), lambda j, i: (i, j)),
        out_shape=jax.ShapeDtypeStruct((m, n), F32),
        scratch_shapes=scratch,
        compiler_params=_params(2),
        name=name,
    )(x, w)


def _pack_kernel(*refs, shift):
    if shift:
        a_ref, b_ref, _, o_ref = refs
        keep = o_ref.shape[0] - shift
        o_ref[0:keep, :] = a_ref[shift:, :].astype(BF16)
        o_ref[keep:, :] = b_ref[...].astype(BF16)
    else:
        a_ref, o_ref = refs
        o_ref[...] = a_ref[...].astype(BF16)


def _gate_rows_kernel(w_ref, o_ref):
    rank = w_ref.shape[0]
    o_ref[0:rank, :] = w_ref[...].astype(BF16)
    o_ref[rank:, :] = jnp.zeros((o_ref.shape[0] - rank, o_ref.shape[1]), BF16)


def _pack_in_weights(w_in, c0, rank):
    depth, k, n = w_in.shape
    wt = jnp.transpose(w_in, (0, 2, 1))
    n_hi = n - rank - c0
    assert c0 % LANES == 0 and n_hi % LANES == 0 and rank % BF16_ROWS == 0 and rank < LANES
    tr = _divisor(_gcd(c0, n_hi), 256, LANES)
    out_shape = jax.ShapeDtypeStruct((depth, c0 + n_hi, k), BF16)
    lo = pl.pallas_call(
        functools.partial(_pack_kernel, shift=0),
        grid=(depth, c0 // tr),
        in_specs=[pl.BlockSpec((None, tr, k), lambda l, j: (l, j, 0))],
        out_specs=pl.BlockSpec((None, tr, k), lambda l, j: (l, j, 0)),
        out_shape=out_shape,
        compiler_params=_params(2),
        name="pack_w_in_lo",
    )(wt)
    j0 = c0 // tr
    main = pl.pallas_call(
        functools.partial(_pack_kernel, shift=rank),
        grid=(depth, n_hi // tr),
        in_specs=[pl.BlockSpec((None, tr, k), lambda l, j: (l, j0 + j, 0)),
                  pl.BlockSpec((None, rank, k), lambda l, j: (l, (c0 + (j + 1) * tr) // rank, 0)),
                  pl.BlockSpec(memory_space=pl.ANY)],
        out_specs=pl.BlockSpec((None, tr, k), lambda l, j: (l, j0 + j, 0)),
        out_shape=out_shape,
        input_output_aliases={2: 0},
        compiler_params=_params(2),
        name="pack_w_in_hi",
    )(wt, wt, lo)
    gate = pl.pallas_call(
        _gate_rows_kernel,
        grid=(depth,),
        in_specs=[pl.BlockSpec((None, rank, k), lambda l: (l, c0 // rank, 0))],
        out_specs=pl.BlockSpec((None, LANES, k), lambda l: (l, 0, 0)),
        out_shape=jax.ShapeDtypeStruct((depth, LANES, k), BF16),
        compiler_params=_params(1),
        name="pack_w_in_gate",
    )(wt)
    return main, gate


def _ln_kernel(*refs, alpha, first_blocks, two_in, two_out):
    refs = list(refs)
    i = pl.program_id(0)
    in_first = i < first_blocks
    if two_in:
        xa_ref, xb_ref = refs.pop(0), refs.pop(0)
        x = jnp.where(in_first, xa_ref[...], xb_ref[...])
    else:
        x = refs.pop(0)[...]
    y_ref, g_ref, b_ref = refs[:3]
    z = alpha * x + y_ref[...]
    mu = jnp.mean(z, axis=-1, keepdims=True)
    zc = z - mu
    var = jnp.mean(zc * zc, axis=-1, keepdims=True)
    o = zc * lax.rsqrt(var + EPS) * g_ref[...] + b_ref[...]
    if two_out:
        oa_ref, ob_ref = refs[3:]

        @pl.when(in_first)
        def _():
            oa_ref[...] = o

        @pl.when(jnp.logical_not(in_first))
        def _():
            ob_ref[...] = o
    else:
        o32_ref, o16_ref = refs[3:]
        o32_ref[...] = o
        o16_ref[...] = o.astype(BF16)


def _residual_ln(x, y, g, b, alpha, *, n_first, split_out, name):
    n, d = y.shape
    tr = _divisor(_gcd(n_first, n - n_first), 256, BF16_ROWS)
    fb = n_first // tr
    row = pl.BlockSpec((tr, d), lambda i: (i, 0))
    first = pl.BlockSpec((tr, d), lambda i: (jnp.minimum(i, fb - 1), 0))
    second = pl.BlockSpec((tr, d), lambda i: (jnp.maximum(i - fb, 0), 0))
    vec = pl.BlockSpec((1, d), lambda i: (0, 0))
    two_in = isinstance(x, tuple)
    x_args, x_specs = (list(x), [first, second]) if two_in else ([x], [row])
    if split_out:
        out_specs = [first, second]
        out_shape = [jax.ShapeDtypeStruct((n_first, d), F32), jax.ShapeDtypeStruct((n - n_first, d), F32)]
    else:
        out_specs = [row, row]
        out_shape = [jax.ShapeDtypeStruct((n, d), F32), jax.ShapeDtypeStruct((n, d), BF16)]
    return pl.pallas_call(
        functools.partial(_ln_kernel, alpha=alpha, first_blocks=fb, two_in=two_in, two_out=split_out),
        grid=(n // tr,),
        in_specs=x_specs + [row, vec, vec],
        out_specs=out_specs,
        out_shape=out_shape,
        compiler_params=_params(1),
        name=name,
    )(*x_args, y, g.reshape(1, d), b.reshape(1, d))


def _chunk_prep(q, k, lg, c, groups):
    r = q.shape[0]
    shift = c.bit_length() - 1
    ri = lax.broadcasted_iota(jnp.int32, (r, r), 0)
    ci = lax.broadcasted_iota(jnp.int32, (r, r), 1)
    tri = ci <= ri
    grp = lax.broadcasted_iota(jnp.int32, (r, 1), 0) >> shift
    if groups > 1:
        tri = tri & ((ri >> shift) == (ci >> shift))
    b = jnp.dot(tri.astype(F32), lg, precision=HIGHEST, preferred_element_type=F32)
    b_ref = b[c // 2:c // 2 + 1]
    b_last = b[c - 1:c]
    for g in range(1, groups):
        b_ref = jnp.where(grp == g, b[g * c + c // 2:g * c + c // 2 + 1], b_ref)
        b_last = jnp.where(grp == g, b[g * c + c - 1:g * c + c], b_last)
    qs = (q * jnp.exp(b - b_ref)).astype(BF16)
    ks = (k * jnp.exp(b_ref - b)).astype(BF16)
    q_in = q * jnp.exp(b)
    k_out = k * jnp.exp(b_last - b)
    return tri, grp, b, qs, ks, q_in, k_out


def _chunk_head(tri, grp, b, qs, ks, q_in, k_out, v, c, groups, s_in, s_out, decay_cols=None):
    transposed = decay_cols is None
    dk = qs.shape[1]
    dv = v.shape[1]
    vb = v.astype(BF16)
    scores = lax.dot_general(qs, ks, _NT, preferred_element_type=F32)
    scores = jnp.where(tri, scores, 0.0)
    if transposed and groups == 2:
        rows = qs.shape[0]
        later = lax.broadcasted_iota(jnp.int32, (rows, rows), 0) >= c
        earlier = lax.broadcasted_iota(jnp.int32, (rows, rows), 1) < c
        cross = lax.dot_general(q_in.astype(BF16), k_out.astype(BF16), _NT, preferred_element_type=F32)
        scores = scores + jnp.where(later & earlier, cross, 0.0)
    o = jnp.dot(scores.astype(BF16), vb, preferred_element_type=F32)
    if transposed:
        st_in, st_out = s_in(0), s_out(0)
        decay = jnp.exp(b[c - 1:c])
        if groups == 2:
            decay_b = jnp.exp(b[2 * c - 1:2 * c])
            q_in = jnp.where(grp == 1, q_in * decay, q_in)
            k_out = jnp.where(grp == 1, k_out, k_out * decay_b)
            decay = decay * decay_b
        else:
            assert groups == 1
        o = o + lax.dot_general(q_in.astype(BF16), st_in[...].astype(BF16), _NT, preferred_element_type=F32)
        upd_t = lax.dot_general(vb, k_out.astype(BF16), _TN, preferred_element_type=F32)
        for j in range(dv // LANES):
            sl = slice(j * LANES, (j + 1) * LANES)
            st_out[sl, :] = st_in[sl, :] * decay + upd_t[sl, :]
        return o
    q_in = q_in.astype(BF16)
    for g in range(groups):
        og = jnp.dot(q_in, s_in(g)[...].astype(BF16), preferred_element_type=F32)
        o = o + (og if groups == 1 else jnp.where(grp == g, og, 0.0))
        kg = k_out if groups == 1 else jnp.where(grp == g, k_out, 0.0)
        upd = lax.dot_general(kg.astype(BF16), vb, _TN, preferred_element_type=F32)
        for j in range(dv // LANES):
            sl = slice(j * LANES, (j + 1) * LANES)
            s_out(g)[:, sl] = s_in(g)[:, sl] * decay_cols[g] + upd[:, sl]
    return o


def _decay_columns(b, c, groups):
    width = b.shape[1]
    ones = jnp.ones((SUBLANES, LANES), F32)
    row0 = lax.broadcasted_iota(jnp.int32, (SUBLANES, 1), 0) == 0
    cols = []
    for g in range(groups):
        bl = jnp.where(row0, jnp.broadcast_to(b[g * c + c - 1:g * c + c], (SUBLANES, width)), 0.0)
        cols.append(jnp.exp(lax.dot_general(bl, ones, _TN, precision=HIGHEST, preferred_element_type=F32)))
    return cols


def _rms(o, gain):
    return o * lax.rsqrt(jnp.mean(o * o, axis=-1, keepdims=True) + EPS) * gain


def _gla_kernel(q_ref, k_ref, v_ref, r_ref, ma_ref, lr_ref, wg_ref, bg_ref, gn_ref, *rest,
                c, groups, chunks, n_tblk, scale, has_state, n_aliased):
    rest = list(rest)
    if has_state:
        s0_ref = rest.pop(0)
    del rest[:n_aliased]
    if has_state:
        ya_ref, so_ref = rest
        s_in = lambda g: s0_ref.at[g, 0]
        s_out = lambda g: so_ref.at[g, 0]
    else:
        ya_ref, so_ref, s_scr = rest
        s_in = s_out = lambda g: s_scr.at[g]
        tb = pl.program_id(2)

        @pl.when(tb == 0)
        def _():
            s_scr[...] = jnp.zeros_like(s_scr)

    if not has_state and chunks % 2 == 0:
        groups, chunks = 2, chunks // 2
    rows_per_chunk = groups * c

    def chunk(ci, carry):
        rows = pl.ds(pl.multiple_of(ci * rows_per_chunk, rows_per_chunk), rows_per_chunk)
        gate = jnp.dot(lr_ref[rows, :].astype(BF16), wg_ref[...], preferred_element_type=F32) + bg_ref[...]
        lg = _log_sigmoid(gate) * (1.0 / GLA_GATE_NORM)
        prep = _chunk_prep(q_ref[rows, :] * scale, k_ref[rows, :], lg, c, groups)
        cols = _decay_columns(prep[2], c, groups) if has_state else None
        o = _chunk_head(*prep, v_ref[rows, :], c, groups, s_in, s_out, cols)
        y = _rms(o, gn_ref[...]) * _silu(r_ref[rows, :])
        ya_ref[rows, :] = _sigmoid(ma_ref[rows, :]) * y
        return carry

    lax.fori_loop(0, chunks, chunk, 0)

    if not has_state:
        @pl.when(tb == n_tblk - 1)
        def _():
            so_ref[0, 0] = s_scr[0].T


def _gla_branch(p, lr, wg, bg, gn, state, s_all, *, depth, layer, row0, nb, t, c, heads, dk, dv, col, name):
    has_state = state is not None
    if has_state:
        assert t == c and SUBLANES % c == 0
        groups, chunks = SUBLANES // c, 1
    else:
        groups, chunks = 1, min(4, t // c)
    rblk = groups * c * chunks
    n_tblk = t // (c * chunks)
    assert nb % groups == 0 and row0 % rblk == 0 and t % (c * chunks) == 0
    rb0 = row0 // rblk

    def rowspec(width, off):
        assert off % width == 0
        return pl.BlockSpec((rblk, width), lambda b, h, tb: (rb0 + b * n_tblk + tb, off // width + h))

    in_specs = [rowspec(dk, col["q"]), rowspec(dk, col["k"]), rowspec(dv, col["v"]), rowspec(dv, col["r"]),
                rowspec(dv, col["ma"]),
                pl.BlockSpec((rblk, LANES), lambda b, h, tb: (rb0 + b * n_tblk + tb, 0)),
                pl.BlockSpec((LANES, dk), lambda b, h, tb: (0, h)),
                pl.BlockSpec((1, dk), lambda b, h, tb: (0, h)),
                pl.BlockSpec((1, dv), lambda b, h, tb: (0, 0))]
    args = [p, p, p, p, p, lr, wg, bg, gn]
    state_spec = pl.BlockSpec((None, groups, 1, dk, dv), lambda b, h, tb: (layer, b, h, 0, 0))
    scratch = []
    if has_state:
        in_specs.append(state_spec)
        args.append(state)
    else:
        scratch = [pltpu.VMEM((1, dv, dk), F32)]
    aliases = {}
    if s_all is not None:
        aliases[len(args)] = 1
        in_specs.append(pl.BlockSpec(memory_space=pl.ANY))
        args.append(s_all)
    ya, s_new = pl.pallas_call(
        functools.partial(_gla_kernel, c=c, groups=groups, chunks=chunks, n_tblk=n_tblk,
                          scale=dk ** -0.5, has_state=has_state, n_aliased=len(aliases)),
        grid=(nb // groups, heads, n_tblk),
        in_specs=in_specs,
        out_specs=[pl.BlockSpec((rblk, dv), lambda b, h, tb: (b * n_tblk + tb, h)), state_spec],
        out_shape=[jax.ShapeDtypeStruct((nb * t, heads * dv), F32),
                   jax.ShapeDtypeStruct((depth, nb, heads, dk, dv), F32)],
        input_output_aliases=aliases,
        scratch_shapes=scratch,
        compiler_params=_params(3),
        name=name,
    )(*args)
    return ya, s_new


def _hg_kernel(hq_ref, hf_ref, hi_ref, hg_ref, mb_ref, ya_ref, lb_ref, gn_ref, *rest,
               layer, c, groups, chunks, n_tblk, heads, dk, dv, scale, has_state, n_aliased):
    rest = list(rest)
    if has_state:
        s0_ref = rest.pop(0)
    del rest[:n_aliased]
    if has_state:
        out_ref, so_ref = rest
        s_in = lambda h: (lambda g: s0_ref.at[g, h])
        s_out = lambda h: (lambda g: so_ref.at[g, h])
    else:
        out_ref, so_ref, s_scr = rest
        s_in = s_out = lambda h: (lambda g: s_scr.at[h])
        tb = pl.program_id(1)

        @pl.when(tb == 0)
        def _():
            s_scr[...] = jnp.zeros_like(s_scr)

    lbw = lb_ref[...]
    e = jnp.exp(lbw - jnp.max(lbw, axis=0, keepdims=True))
    sm = e / jnp.sum(e, axis=0, keepdims=True)
    cum = sm[0:1]
    first = cum
    for i in range(1, layer + 1):
        cum = cum + sm[i:i + 1]
    lb = cum - first
    log_lb = jnp.log(lb)
    log_1m = jnp.log1p(-lb)
    one_m = 1.0 - lb

    if not has_state and chunks % 2 == 0:
        groups, chunks = 2, chunks // 2
    rows_per_chunk = groups * c

    def chunk(ci, carry):
        rows = pl.ds(pl.multiple_of(ci * rows_per_chunk, rows_per_chunk), rows_per_chunk)
        zf = hf_ref[rows, :]
        ez = jnp.exp(-jnp.abs(zf))
        x2 = log_1m + (jnp.minimum(zf, 0.0) - jnp.log1p(ez))
        lg = jnp.maximum(log_lb, x2) + jnp.log1p(jnp.exp(-jnp.abs(log_lb - x2)))
        kk = one_m * (jnp.where(zf >= 0.0, ez, 1.0) / (1.0 + ez))
        qq = _silu(hq_ref[rows, :]) * scale
        tri, grp, b, qs, ks, q_in, k_out = _chunk_prep(qq, kk, lg, c, groups)
        for h in range(heads):
            sk = slice(h * dk, (h + 1) * dk)
            sv = slice(h * dv, (h + 1) * dv)
            head_cols = _decay_columns(b[:, sk], c, groups) if has_state else None
            o = _chunk_head(tri, grp, b[:, sk], qs[:, sk], ks[:, sk], q_in[:, sk], k_out[:, sk],
                            hi_ref[rows, sv], c, groups, s_in(h), s_out(h), head_cols)
            y = _rms(o, gn_ref[...]) * _silu(hg_ref[rows, sv])
            out_ref[rows, sv] = (ya_ref[rows, sv] + _sigmoid(mb_ref[rows, sv]) * y).astype(BF16)
        return carry

    lax.fori_loop(0, chunks, chunk, 0)

    if not has_state:
        @pl.when(tb == n_tblk - 1)
        def _():
            for h in range(heads):
                so_ref[0, h] = s_scr[h].T


def _hg_branch(p, ya, lower_bounds, gn, state, merged_all, s_all, *, layer, row0, nb, t, c, heads, dk, dv, col,
               name):
    has_state = state is not None
    if has_state:
        assert t == c and BF16_ROWS % c == 0
        groups, chunks = BF16_ROWS // c, 1
    else:
        groups, chunks = 1, min(2, t // c)
    rblk = groups * c * chunks
    n_tblk = t // (c * chunks)
    width = heads * dk
    assert heads * dv == width and nb % groups == 0 and row0 % rblk == 0 and t % (c * chunks) == 0
    rb0 = row0 // rblk

    def rowspec(off):
        assert off % width == 0
        return pl.BlockSpec((rblk, width), lambda b, tb: (rb0 + b * n_tblk + tb, off // width))

    local = pl.BlockSpec((rblk, width), lambda b, tb: (b * n_tblk + tb, 0))
    depth = lower_bounds.shape[0]
    in_specs = [rowspec(col["hq"]), rowspec(col["hf"]), rowspec(col["hi"]), rowspec(col["hg"]),
                rowspec(col["mb"]), local,
                pl.BlockSpec((depth, width), lambda b, tb: (0, 0)),
                pl.BlockSpec((1, dv), lambda b, tb: (0, 0))]
    args = [p, p, p, p, p, ya, lower_bounds, gn]
    state_spec = pl.BlockSpec((None, groups, heads, dk, dv), lambda b, tb: (layer, b, 0, 0, 0))
    scratch = []
    if has_state:
        in_specs.append(state_spec)
        args.append(state)
    else:
        scratch = [pltpu.VMEM((heads, dv, dk), F32)]
    aliases = {}
    for out_idx, prior in enumerate((merged_all, s_all)):
        if prior is not None:
            aliases[len(args)] = out_idx
            in_specs.append(pl.BlockSpec(memory_space=pl.ANY))
            args.append(prior)
    merged, s_new = pl.pallas_call(
        functools.partial(_hg_kernel, layer=layer, c=c, groups=groups, chunks=chunks, n_tblk=n_tblk,
                          heads=heads, dk=dk, dv=dv, scale=dk ** -0.5, has_state=has_state,
                          n_aliased=len(aliases)),
        grid=(nb // groups, n_tblk),
        in_specs=in_specs,
        out_specs=[pl.BlockSpec((rblk, width), lambda b, tb: (rb0 + b * n_tblk + tb, 0)), state_spec],
        out_shape=[jax.ShapeDtypeStruct((p.shape[0], width), BF16),
                   jax.ShapeDtypeStruct((depth, nb, heads, dk, dv), F32)],
        input_output_aliases=aliases,
        scratch_shapes=scratch,
        compiler_params=_params(2),
        name=name,
    )(*args)
    return merged, s_new


def _up_kernel(x_ref, wa_ref, wg_ref, cwa_ref, cwg_ref, cba_ref, cbg_ref, h0a_ref, h0g_ref, h1a_ref, h1g_ref,
               h_ref, taila_ref, tailg_ref, usa_ref, usg_ref, sa_scr, sg_scr, wa_bf16, wg_bf16,
               *, tm, n_tiles, n_row_tiles, n_prompt_tiles, tiles_per_seq, ts):
    s = pl.program_id(0)
    hist = SUBLANES

    @pl.when(s == 0)
    def _():
        sa_scr[...] = jnp.zeros_like(sa_scr)
        sg_scr[...] = jnp.zeros_like(sg_scr)

    @pl.when(jnp.logical_and(s % n_row_tiles == 0, s < n_tiles))
    def _():
        wa_bf16[...] = wa_ref[...].astype(BF16)
        wg_bf16[...] = wg_ref[...].astype(BF16)

    ip = jnp.maximum(s - 1, 0) % n_row_tiles
    is_sample = ip >= n_prompt_tiles
    seq_start = jnp.logical_and(jnp.logical_not(is_sample), (ip % tiles_per_seq) == 0)
    row = lax.broadcasted_iota(jnp.int32, (tm, 1), 0)
    pos = jnp.where(is_sample, row & (ts - 1), jnp.where(seq_start, row, CONV_W))
    keep1 = pos >= 1
    keep2 = pos >= 2

    def epilogue(cw_ref, cb_ref, h0_ref, h1_ref, tail_ref, us_ref, scr):
        u = scr[hist:hist + tm, :]
        u1 = jnp.where(keep1, scr[hist - 1:hist - 1 + tm, :], 0.0)
        u2 = jnp.where(keep2, scr[hist - 2:hist - 2 + tm, :], 0.0)
        u1 = u1 + jnp.where(is_sample, h1_ref[...], 0.0)
        u2 = u2 + jnp.where(is_sample, h0_ref[...], 0.0)
        uc = u2 * cw_ref[0:1, :] + u1 * cw_ref[1:2, :] + u * cw_ref[2:3, :] + cb_ref[...]
        tail = scr[tm:tm + hist, :]
        scr[0:hist, :] = tail
        tail_ref[0] = tail
        us_ref[...] = u
        return uc

    a = epilogue(cwa_ref, cba_ref, h0a_ref, h1a_ref, taila_ref, usa_ref, sa_scr)
    g = epilogue(cwg_ref, cbg_ref, h0g_ref, h1g_ref, tailg_ref, usg_ref, sg_scr)
    h_ref[...] = (_silu(a) * g).astype(BF16)

    x = x_ref[...]
    sa_scr[hist:hist + tm, :] = jnp.dot(x, wa_bf16[...], preferred_element_type=F32)
    sg_scr[hist:hist + tm, :] = jnp.dot(x, wg_bf16[...], preferred_element_type=F32)


def _up_conv_gate(xb, w, layer, cw, cb, h0, h1, *, n_prompt, t_prompt, ts, name):
    n, d = xb.shape
    f = w.shape[2] // 2
    n_sample = n - n_prompt
    tm = _divisor(_gcd(t_prompt, n_sample), 512, BF16_ROWS)
    assert tm % ts == 0 and ts & (ts - 1) == 0 and ts >= CONV_W - 1 and tm >= SUBLANES
    tn = _divisor(f, 256, LANES)
    ncb = f // tn
    npt = n_prompt // tm
    nrt = n // tm
    n_tiles = ncb * nrt

    def cur(s):
        c = jnp.minimum(s, n_tiles - 1)
        return c // nrt, c % nrt

    def prev(s):
        c = jnp.maximum(s - 1, 0)
        return c // nrt, c % nrt

    def prev_sample(s, off):
        j, i = prev(s)
        return jnp.maximum(i - npt, 0), off + j

    in_specs = [pl.BlockSpec((tm, d), lambda s: (cur(s)[1], 0)),
                pl.BlockSpec((None, d, tn), lambda s: (layer, 0, cur(s)[0])),
                pl.BlockSpec((None, d, tn), lambda s: (layer, 0, ncb + cur(s)[0])),
                pl.BlockSpec((CONV_W, tn), lambda s: (0, prev(s)[0])),
                pl.BlockSpec((CONV_W, tn), lambda s: (0, ncb + prev(s)[0])),
                pl.BlockSpec((1, tn), lambda s: (0, prev(s)[0])),
                pl.BlockSpec((1, tn), lambda s: (0, ncb + prev(s)[0])),
                pl.BlockSpec((tm, tn), lambda s: prev_sample(s, 0)),
                pl.BlockSpec((tm, tn), lambda s: prev_sample(s, ncb)),
                pl.BlockSpec((tm, tn), lambda s: prev_sample(s, 0)),
                pl.BlockSpec((tm, tn), lambda s: prev_sample(s, ncb))]
    tail_spec = pl.BlockSpec((1, SUBLANES, tn), lambda s: (prev(s)[1], 0, prev(s)[0]))
    us_spec = pl.BlockSpec((tm, tn), lambda s: prev_sample(s, 0))
    out_specs = [pl.BlockSpec((tm, tn), lambda s: (prev(s)[1], prev(s)[0])), tail_spec, tail_spec,
                 us_spec, us_spec]
    out_shape = [jax.ShapeDtypeStruct((n, f), BF16),
                 jax.ShapeDtypeStruct((nrt, SUBLANES, f), F32), jax.ShapeDtypeStruct((nrt, SUBLANES, f), F32),
                 jax.ShapeDtypeStruct((n_sample, f), F32), jax.ShapeDtypeStruct((n_sample, f), F32)]
    return pl.pallas_call(
        functools.partial(_up_kernel, tm=tm, n_tiles=n_tiles, n_row_tiles=nrt, n_prompt_tiles=npt,
                          tiles_per_seq=t_prompt // tm, ts=ts),
        grid=(n_tiles + 1,),
        in_specs=in_specs,
        out_specs=out_specs,
        out_shape=out_shape,
        scratch_shapes=[pltpu.VMEM((tm + SUBLANES, tn), F32), pltpu.VMEM((tm + SUBLANES, tn), F32),
                        pltpu.VMEM((d, tn), BF16), pltpu.VMEM((d, tn), BF16)],
        compiler_params=_params(1),
        name=name,
    )(xb, w, w, cw, cw, cb, cb, h0, h0, h1, h1), tm


def _gcd(a, b):
    while b:
        a, b = b, a % b
    return a


def kernel(x_prompt, x_sample, state_gla, state_hgrn, state_ffn_conv, w_in, w_gla_gate2, b_gla_gate, g_gla_norm,
           lower_bounds, g_hgrn_norm, w_out, ln1_g, ln1_b, w_up, conv_w, conv_b, w_down, ln2_g, ln2_b):
    bp, tp, d = x_prompt.shape
    bs, ts, _ = x_sample.shape
    depth, _, heads, dk, dv = state_gla.shape
    _, _, hheads, hk, hv = state_hgrn.shape
    rank = w_gla_gate2.shape[1]
    f2 = w_up.shape[2]
    f = f2 // 2
    qk, vw, hgk, hgv = heads * dk, heads * dv, hheads * hk, hheads * hv
    c0 = 2 * qk + 2 * vw
    assert conv_w.shape[1] == CONV_W and rank <= LANES
    n_p, n_s = bp * tp, bs * ts
    alpha = (2.0 * depth) ** 0.25
    col = {"q": 0, "k": qk, "v": 2 * qk, "r": 2 * qk + vw,
           "hq": c0, "hf": c0 + hgk, "hi": c0 + 2 * hgk, "hg": c0 + 2 * hgk + hgv,
           "ma": c0 + 2 * hgk + 2 * hgv, "mb": c0 + 2 * hgk + 2 * hgv + d}
    gla_c_s = GLA_CHUNK if ts % GLA_CHUNK == 0 else ts
    hg_c_s = HG_CHUNK if ts % HG_CHUNK == 0 else ts
    gla_c_p = GLA_CHUNK if tp % GLA_CHUNK == 0 else tp
    hg_c_p = HG_CHUNK if tp % HG_CHUNK == 0 else tp

    x = (x_prompt.reshape(n_p, d), x_sample.reshape(n_s, d))
    xb = jnp.concatenate([x[0].astype(BF16), x[1].astype(BF16)], axis=0)
    gla_p = gla_s = hgr_p = hgr_s = None
    new_conv_p, new_conv_s = [], []
    w_main, w_lr = _pack_in_weights(w_in, c0, rank)
    w_down_bf16 = w_down.astype(BF16)
    for l in range(depth):
        wg2 = jnp.pad(w_gla_gate2[l], ((0, LANES - rank), (0, 0))).astype(BF16)
        p = _matmul(xb, w_main, l, tm_cap=1088, tn_cap=1024, w_is_nk=True, name=f"in_proj_{l}")
        lr = _matmul(xb, w_lr, l, tm_cap=1088, tn_cap=LANES, w_is_nk=True, name=f"in_proj_lr_{l}")

        gla_args = dict(depth=depth, layer=l, heads=heads, dk=dk, dv=dv, col=col)
        bg = b_gla_gate[l].reshape(1, qk)
        gn = g_gla_norm[l].reshape(1, dv)
        ya_p, gla_p = _gla_branch(p, lr, wg2, bg, gn, None, gla_p, row0=0, nb=bp, t=tp, c=gla_c_p,
                                  name=f"gla_prompt_{l}", **gla_args)
        ya_s, gla_s = _gla_branch(p, lr, wg2, bg, gn, state_gla, gla_s, row0=n_p, nb=bs, t=ts, c=gla_c_s,
                                  name=f"gla_sample_{l}", **gla_args)
        hg_args = dict(layer=l, heads=hheads, dk=hk, dv=hv, col=col)
        hgn = g_hgrn_norm[l].reshape(1, hv)
        merged, hgr_p = _hg_branch(p, ya_p, lower_bounds, hgn, None, None, hgr_p, row0=0, nb=bp, t=tp,
                                   c=hg_c_p, name=f"hgrn_prompt_{l}", **hg_args)
        merged, hgr_s = _hg_branch(p, ya_s, lower_bounds, hgn, state_hgrn, merged, hgr_s, row0=n_p, nb=bs,
                                   t=ts, c=hg_c_s, name=f"hgrn_sample_{l}", **hg_args)

        mix = _matmul(merged, w_out, l, tm_cap=1088, tn_cap=512, name=f"out_proj_{l}")
        x, xb = _residual_ln(x, mix, ln1_g[l], ln1_b[l], alpha, n_first=n_p, split_out=False, name=f"ln1_{l}")

        sc = state_ffn_conv[l]
        zero = jnp.zeros_like(sc[:, 0])
        h0 = jnp.stack([sc[:, 0], sc[:, 1]] + [zero] * (ts - 2), axis=1).reshape(n_s, f2)
        h1 = jnp.stack([sc[:, 1]] + [zero] * (ts - 1), axis=1).reshape(n_s, f2)
        (hb, tail_a, tail_g, us_a, us_g), tm = _up_conv_gate(
            xb, w_up, l, conv_w[l], conv_b[l].reshape(1, f2), h0, h1,
            n_prompt=n_p, t_prompt=tp, ts=ts, name=f"up_conv_{l}")
        last_tiles = (jnp.arange(bp) + 1) * (tp // tm) - 1
        keep = slice(SUBLANES - (CONV_W - 1), SUBLANES)
        new_conv_p.append(jnp.concatenate([tail_a[last_tiles][:, keep], tail_g[last_tiles][:, keep]], axis=-1))
        new_conv_s.append(jnp.concatenate([us_a.reshape(bs, ts, f)[:, ts - (CONV_W - 1):],
                                           us_g.reshape(bs, ts, f)[:, ts - (CONV_W - 1):]], axis=-1))

        y = _matmul(hb, w_down_bf16, l, tm_cap=512, tn_cap=512, name=f"down_proj_{l}")
        if l < depth - 1:
            x, xb = _residual_ln(x, y, ln2_g[l], ln2_b[l], alpha, n_first=n_p, split_out=False, name=f"ln2_{l}")
        else:
            y_prompt, y_sample = _residual_ln(x, y, ln2_g[l], ln2_b[l], alpha, n_first=n_p, split_out=True,
                                              name=f"ln2_{l}")

    return (y_prompt.reshape(bp, tp, d), y_sample.reshape(bs, ts, d), gla_p, gla_s, hgr_p, hgr_s,
            jnp.stack(new_conv_p), jnp.stack(new_conv_s))
```

```python
import functools

import jax
import jax.numpy as jnp
from jax import lax
from jax.experimental import pallas as pl
from jax.experimental.pallas import tpu as pltpu

GLA_CHUNK = 64
HG_CHUNK = 32
GLA_GATE_NORM = 16.0
EPS = 1e-5
CONV_W = 3

LANES = 128
SUBLANES = 8
BF16_ROWS = 16
V7X_VMEM_BYTES = 64 * 1024 * 1024
VMEM_LIMIT = V7X_VMEM_BYTES - 8 * 1024 * 1024

F32 = jnp.float32
BF16 = jnp.bfloat16
HIGHEST = lax.Precision.HIGHEST
_NT = (((1,), (1,)), ((), ()))
_TN = (((0,), (0,)), ((), ()))


def _divisor(n, cap, mult):
    if n <= cap:
        return n
    best = None
    for d in range(mult, cap + 1, mult):
        if n % d == 0:
            best = d
    assert best is not None, (n, cap, mult)
    return best


def _params(n_axes):
    return pltpu.CompilerParams(dimension_semantics=("arbitrary",) * n_axes, vmem_limit_bytes=VMEM_LIMIT)


def _sigmoid(x):
    return 1.0 / (1.0 + jnp.exp(-x))


def _silu(x):
    return x * _sigmoid(x)


def _log_sigmoid(x):
    return jnp.minimum(x, 0.0) - jnp.log1p(jnp.exp(-jnp.abs(x)))


def _split3(x):
    hi = x.astype(BF16)
    rest = x - hi.astype(F32)
    mid = rest.astype(BF16)
    lo = (rest - mid.astype(F32)).astype(BF16)
    return hi, mid, lo


def _mm_kernel(x_ref, w_ref, o_ref, *w_bf16, w_is_nk):
    if w_bf16:
        (wb_ref,) = w_bf16

        @pl.when(pl.program_id(1) == 0)
        def _():
            wb_ref[...] = w_ref[...].astype(BF16)

        w = wb_ref[...]
    else:
        w = w_ref[...]
    if w_is_nk:
        o_ref[...] = lax.dot_general(x_ref[...], w, _NT, preferred_element_type=F32)
    else:
        o_ref[...] = jnp.dot(x_ref[...], w, preferred_element_type=F32)


def _matmul(x, w, layer, *, tm_cap, tn_cap, name, w_is_nk=False):
    m, k = x.shape
    n = w.shape[1] if w_is_nk else w.shape[2]
    tm = _divisor(m, tm_cap, BF16_ROWS)
    tn = _divisor(n, tn_cap, LANES)
    w_block = (None, tn, k) if w_is_nk else (None, k, tn)
    w_index = (lambda j, i: (layer, j, 0)) if w_is_nk else (lambda j, i: (layer, 0, j))
    scratch = [pltpu.VMEM(w_block[1:], BF16)] if w.dtype == F32 else []
    return pl.pallas_call(
        functools.partial(_mm_kernel, w_is_nk=w_is_nk),
        grid=(n // tn, m // tm),
        in_specs=[pl.BlockSpec((tm, k), lambda j, i: (i, 0)), pl.BlockSpec(w_block, w_index)],
        out_specs=pl.BlockSpec((tm, tn), lambda j, i: (i, j)),
        out_shape=jax.ShapeDtypeStruct((m, n), F32),
        scratch_shapes=scratch,
        compiler_params=_params(2),
        name=name,
    )(x, w)


def _pack_kernel(*refs, shift):
    if shift:
        a_ref, b_ref, _, o_ref = refs
        keep = o_ref.shape[0] - shift
        o_ref[0:keep, :] = a_ref[shift:, :].astype(BF16)
        o_ref[keep:, :] = b_ref[...].astype(BF16)
    else:
        a_ref, o_ref = refs
        o_ref[...] = a_ref[...].astype(BF16)


def _gate_rows_kernel(w_ref, o_ref):
    rank = w_ref.shape[0]
    o_ref[0:rank, :] = w_ref[...].astype(BF16)
    o_ref[rank:, :] = jnp.zeros((o_ref.shape[0] - rank, o_ref.shape[1]), BF16)


def _pack_in_weights(w_in, c0, rank):
    depth, k, n = w_in.shape
    wt = jnp.transpose(w_in, (0, 2, 1))
    n_hi = n - rank - c0
    assert c0 % LANES == 0 and n_hi % LANES == 0 and rank % BF16_ROWS == 0 and rank < LANES
    tr = _divisor(_gcd(c0, n_hi), 256, LANES)
    out_shape = jax.ShapeDtypeStruct((depth, c0 + n_hi, k), BF16)
    lo = pl.pallas_call(
        functools.partial(_pack_kernel, shift=0),
        grid=(depth, c0 // tr),
        in_specs=[pl.BlockSpec((None, tr, k), lambda l, j: (l, j, 0))],
        out_specs=pl.BlockSpec((None, tr, k), lambda l, j: (l, j, 0)),
        out_shape=out_shape,
        compiler_params=_params(2),
        name="pack_w_in_lo",
    )(wt)
    j0 = c0 // tr
    main = pl.pallas_call(
        functools.partial(_pack_kernel, shift=rank),
        grid=(depth, n_hi // tr),
        in_specs=[pl.BlockSpec((None, tr, k), lambda l, j: (l, j0 + j, 0)),
                  pl.BlockSpec((None, rank, k), lambda l, j: (l, (c0 + (j + 1) * tr) // rank, 0)),
                  pl.BlockSpec(memory_space=pl.ANY)],
        out_specs=pl.BlockSpec((None, tr, k), lambda l, j: (l, j0 + j, 0)),
        out_shape=out_shape,
        input_output_aliases={2: 0},
        compiler_params=_params(2),
        name="pack_w_in_hi",
    )(wt, wt, lo)
    gate = pl.pallas_call(
        _gate_rows_kernel,
        grid=(depth,),
        in_specs=[pl.BlockSpec((None, rank, k), lambda l: (l, c0 // rank, 0))],
        out_specs=pl.BlockSpec((None, LANES, k), lambda l: (l, 0, 0)),
        out_shape=jax.ShapeDtypeStruct((depth, LANES, k), BF16),
        compiler_params=_params(1),
        name="pack_w_in_gate",
    )(wt)
    return main, gate


def _ln_kernel(*refs, alpha, first_blocks, two_in, two_out):
    refs = list(refs)
    i = pl.program_id(0)
    in_first = i < first_blocks
    if two_in:
        xa_ref, xb_ref = refs.pop(0), refs.pop(0)
        x = jnp.where(in_first, xa_ref[...], xb_ref[...])
    else:
        x = refs.pop(0)[...]
    y_ref, g_ref, b_ref = refs[:3]
    z = alpha * x + y_ref[...]
    mu = jnp.mean(z, axis=-1, keepdims=True)
    zc = z - mu
    var = jnp.mean(zc * zc, axis=-1, keepdims=True)
    o = zc * lax.rsqrt(var + EPS) * g_ref[...] + b_ref[...]
    if two_out:
        oa_ref, ob_ref = refs[3:]

        @pl.when(in_first)
        def _():
            oa_ref[...] = o

        @pl.when(jnp.logical_not(in_first))
        def _():
            ob_ref[...] = o
    else:
        o32_ref, o16_ref = refs[3:]
        o32_ref[...] = o
        o16_ref[...] = o.astype(BF16)


def _residual_ln(x, y, g, b, alpha, *, n_first, split_out, name):
    n, d = y.shape
    tr = _divisor(_gcd(n_first, n - n_first), 256, BF16_ROWS)
    fb = n_first // tr
    row = pl.BlockSpec((tr, d), lambda i: (i, 0))
    first = pl.BlockSpec((tr, d), lambda i: (jnp.minimum(i, fb - 1), 0))
    second = pl.BlockSpec((tr, d), lambda i: (jnp.maximum(i - fb, 0), 0))
    vec = pl.BlockSpec((1, d), lambda i: (0, 0))
    two_in = isinstance(x, tuple)
    x_args, x_specs = (list(x), [first, second]) if two_in else ([x], [row])
    if split_out:
        out_specs = [first, second]
        out_shape = [jax.ShapeDtypeStruct((n_first, d), F32), jax.ShapeDtypeStruct((n - n_first, d), F32)]
    else:
        out_specs = [row, row]
        out_shape = [jax.ShapeDtypeStruct((n, d), F32), jax.ShapeDtypeStruct((n, d), BF16)]
    return pl.pallas_call(
        functools.partial(_ln_kernel, alpha=alpha, first_blocks=fb, two_in=two_in, two_out=split_out),
        grid=(n // tr,),
        in_specs=x_specs + [row, vec, vec],
        out_specs=out_specs,
        out_shape=out_shape,
        compiler_params=_params(1),
        name=name,
    )(*x_args, y, g.reshape(1, d), b.reshape(1, d))


def _chunk_prep(q, k, lg, c, groups):
    r = q.shape[0]
    shift = c.bit_length() - 1
    ri = lax.broadcasted_iota(jnp.int32, (r, r), 0)
    ci = lax.broadcasted_iota(jnp.int32, (r, r), 1)
    tri = ci <= ri
    grp = lax.broadcasted_iota(jnp.int32, (r, 1), 0) >> shift
    if groups > 1:
        tri = tri & ((ri >> shift) == (ci >> shift))
    if r % BF16_ROWS == 0:
        tri3 = jnp.concatenate([tri.astype(BF16)] * 3, axis=1)
        b = jnp.dot(tri3, jnp.concatenate(_split3(lg), axis=0), preferred_element_type=F32)
    else:
        b = jnp.dot(tri.astype(F32), lg, precision=HIGHEST, preferred_element_type=F32)
    b_ref = b[c // 2:c // 2 + 1]
    b_last = b[c - 1:c]
    for g in range(1, groups):
        b_ref = jnp.where(grp == g, b[g * c + c // 2:g * c + c // 2 + 1], b_ref)
        b_last = jnp.where(grp == g, b[g * c + c - 1:g * c + c], b_last)
    qs = (q * jnp.exp(b - b_ref)).astype(BF16)
    ks = (k * jnp.exp(b_ref - b)).astype(BF16)
    q_in = q * jnp.exp(b)
    k_out = k * jnp.exp(b_last - b)
    return tri, grp, b, qs, ks, q_in, k_out


def _chunk_head(tri, grp, b, qs, ks, q_in, k_out, v, c, groups, s_in, s_out, transposed):
    dk = qs.shape[1]
    dv = v.shape[1]
    vb = v.astype(BF16)
    scores = lax.dot_general(qs, ks, _NT, preferred_element_type=F32)
    scores = jnp.where(tri, scores, 0.0)
    if transposed and groups == 2:
        rows = qs.shape[0]
        later = lax.broadcasted_iota(jnp.int32, (rows, rows), 0) >= c
        earlier = lax.broadcasted_iota(jnp.int32, (rows, rows), 1) < c
        cross = lax.dot_general(q_in.astype(BF16), k_out.astype(BF16), _NT, preferred_element_type=F32)
        scores = scores + jnp.where(later & earlier, cross, 0.0)
    o = jnp.dot(scores.astype(BF16), vb, preferred_element_type=F32)
    if transposed:
        st_in, st_out = s_in(0), s_out(0)
        decay = jnp.exp(b[c - 1:c])
        if groups == 2:
            decay_b = jnp.exp(b[2 * c - 1:2 * c])
            q_in = jnp.where(grp == 1, q_in * decay, q_in)
            k_out = jnp.where(grp == 1, k_out, k_out * decay_b)
            decay = decay * decay_b
        else:
            assert groups == 1
        o = o + lax.dot_general(q_in.astype(BF16), st_in[...].astype(BF16), _NT, preferred_element_type=F32)
        upd_t = lax.dot_general(vb, k_out.astype(BF16), _TN, preferred_element_type=F32)
        for j in range(dv // LANES):
            sl = slice(j * LANES, (j + 1) * LANES)
            st_out[sl, :] = st_in[sl, :] * decay + upd_t[sl, :]
        return o
    def block_diag(x):
        return jnp.concatenate([jnp.where(grp == g, x, 0.0) for g in range(groups)], axis=1).astype(BF16)

    def stacked(sl):
        return jnp.concatenate([s_in(g)[:, sl] for g in range(groups)], axis=0)

    o = o + jnp.dot(block_diag(q_in), stacked(slice(None)).astype(BF16), preferred_element_type=F32)
    upd = lax.dot_general(block_diag(k_out), vb, _TN, preferred_element_type=F32)
    b_last = jnp.concatenate([b[g * c + c - 1:g * c + c] for g in range(groups)], axis=1)
    row = lax.broadcasted_iota(jnp.int32, (BF16_ROWS, 1), 0)
    pieces = jnp.zeros((BF16_ROWS, groups * dk), F32)
    for i, piece in enumerate(_split3(b_last)):
        pieces = jnp.where(row == i, jnp.broadcast_to(piece.astype(F32), pieces.shape), pieces)
    decay_col = jnp.exp(lax.dot_general(pieces.astype(BF16), jnp.ones((BF16_ROWS, LANES), BF16), _TN,
                                        preferred_element_type=F32))
    for j in range(dv // LANES):
        sl = slice(j * LANES, (j + 1) * LANES)
        new = stacked(sl) * decay_col + upd[:, sl]
        for g in range(groups):
            s_out(g)[:, sl] = new[g * dk:(g + 1) * dk]
    return o


def _rms(o, gain):
    return o * lax.rsqrt(jnp.mean(o * o, axis=-1, keepdims=True) + EPS) * gain


def _gla_kernel(q_ref, k_ref, v_ref, r_ref, ma_ref, lr_ref, wg_ref, bg_ref, gn_ref, *rest,
                c, groups, chunks, n_tblk, scale, has_state, n_aliased):
    rest = list(rest)
    if has_state:
        s0_ref = rest.pop(0)
    del rest[:n_aliased]
    if has_state:
        ya_ref, so_ref = rest
        s_in = lambda g: s0_ref.at[g, 0]
        s_out = lambda g: so_ref.at[g, 0]
    else:
        ya_ref, so_ref, s_scr = rest
        s_in = s_out = lambda g: s_scr.at[g]
        tb = pl.program_id(2)

        @pl.when(tb == 0)
        def _():
            s_scr[...] = jnp.zeros_like(s_scr)

    if not has_state and chunks % 2 == 0:
        groups, chunks = 2, chunks // 2
    rows_per_chunk = groups * c

    def chunk(ci, carry):
        rows = pl.ds(pl.multiple_of(ci * rows_per_chunk, rows_per_chunk), rows_per_chunk)
        gate = jnp.dot(lr_ref[rows, :].astype(BF16), wg_ref[...], preferred_element_type=F32) + bg_ref[...]
        lg = _log_sigmoid(gate) * (1.0 / GLA_GATE_NORM)
        prep = _chunk_prep(q_ref[rows, :] * scale, k_ref[rows, :], lg, c, groups)
        o = _chunk_head(*prep, v_ref[rows, :], c, groups, s_in, s_out, not has_state)
        y = _rms(o, gn_ref[...]) * _silu(r_ref[rows, :])
        ya_ref[rows, :] = _sigmoid(ma_ref[rows, :]) * y
        return carry

    lax.fori_loop(0, chunks, chunk, 0)

    if not has_state:
        @pl.when(tb == n_tblk - 1)
        def _():
            so_ref[0, 0] = s_scr[0].T


def _gla_branch(p, lr, wg, bg, gn, state, s_all, *, depth, layer, row0, nb, t, c, heads, dk, dv, col, name):
    has_state = state is not None
    if has_state:
        assert t == c and SUBLANES % c == 0
        groups, chunks = SUBLANES // c, 1
    else:
        groups, chunks = 1, min(4, t // c)
    rblk = groups * c * chunks
    n_tblk = t // (c * chunks)
    assert nb % groups == 0 and row0 % rblk == 0 and t % (c * chunks) == 0
    rb0 = row0 // rblk

    def rowspec(width, off):
        assert off % width == 0
        return pl.BlockSpec((rblk, width), lambda b, h, tb: (rb0 + b * n_tblk + tb, off // width + h))

    in_specs = [rowspec(dk, col["q"]), rowspec(dk, col["k"]), rowspec(dv, col["v"]), rowspec(dv, col["r"]),
                rowspec(dv, col["ma"]),
                pl.BlockSpec((rblk, LANES), lambda b, h, tb: (rb0 + b * n_tblk + tb, 0)),
                pl.BlockSpec((LANES, dk), lambda b, h, tb: (0, h)),
                pl.BlockSpec((1, dk), lambda b, h, tb: (0, h)),
                pl.BlockSpec((1, dv), lambda b, h, tb: (0, 0))]
    args = [p, p, p, p, p, lr, wg, bg, gn]
    state_spec = pl.BlockSpec((None, groups, 1, dk, dv), lambda b, h, tb: (layer, b, h, 0, 0))
    scratch = []
    if has_state:
        in_specs.append(state_spec)
        args.append(state)
    else:
        scratch = [pltpu.VMEM((1, dv, dk), F32)]
    aliases = {}
    if s_all is not None:
        aliases[len(args)] = 1
        in_specs.append(pl.BlockSpec(memory_space=pl.ANY))
        args.append(s_all)
    ya, s_new = pl.pallas_call(
        functools.partial(_gla_kernel, c=c, groups=groups, chunks=chunks, n_tblk=n_tblk,
                          scale=dk ** -0.5, has_state=has_state, n_aliased=len(aliases)),
        grid=(nb // groups, heads, n_tblk),
        in_specs=in_specs,
        out_specs=[pl.BlockSpec((rblk, dv), lambda b, h, tb: (b * n_tblk + tb, h)), state_spec],
        out_shape=[jax.ShapeDtypeStruct((nb * t, heads * dv), F32),
                   jax.ShapeDtypeStruct((depth, nb, heads, dk, dv), F32)],
        input_output_aliases=aliases,
        scratch_shapes=scratch,
        compiler_params=_params(3),
        name=name,
    )(*args)
    return ya, s_new


def _hg_kernel(hq_ref, hf_ref, hi_ref, hg_ref, mb_ref, ya_ref, lb_ref, gn_ref, *rest,
               layer, c, groups, chunks, n_tblk, heads, dk, dv, scale, has_state, n_aliased):
    rest = list(rest)
    if has_state:
        s0_ref = rest.pop(0)
    del rest[:n_aliased]
    if has_state:
        out_ref, so_ref = rest
        s_in = lambda h: (lambda g: s0_ref.at[g, h])
        s_out = lambda h: (lambda g: so_ref.at[g, h])
    else:
        out_ref, so_ref, s_scr = rest
        s_in = s_out = lambda h: (lambda g: s_scr.at[h])
        tb = pl.program_id(1)

        @pl.when(tb == 0)
        def _():
            s_scr[...] = jnp.zeros_like(s_scr)

    lbw = lb_ref[...]
    e = jnp.exp(lbw - jnp.max(lbw, axis=0, keepdims=True))
    sm = e / jnp.sum(e, axis=0, keepdims=True)
    cum = sm[0:1]
    first = cum
    for i in range(1, layer + 1):
        cum = cum + sm[i:i + 1]
    lb = cum - first
    log_lb = jnp.log(lb)
    log_1m = jnp.log1p(-lb)
    one_m = 1.0 - lb

    if not has_state and chunks % 2 == 0:
        groups, chunks = 2, chunks // 2
    rows_per_chunk = groups * c

    def chunk(ci, carry):
        rows = pl.ds(pl.multiple_of(ci * rows_per_chunk, rows_per_chunk), rows_per_chunk)
        zf = hf_ref[rows, :]
        ez = jnp.exp(-jnp.abs(zf))
        x2 = log_1m + (jnp.minimum(zf, 0.0) - jnp.log1p(ez))
        lg = jnp.maximum(log_lb, x2) + jnp.log1p(jnp.exp(-jnp.abs(log_lb - x2)))
        kk = one_m * (jnp.where(zf >= 0.0, ez, 1.0) / (1.0 + ez))
        qq = _silu(hq_ref[rows, :]) * scale
        tri, grp, b, qs, ks, q_in, k_out = _chunk_prep(qq, kk, lg, c, groups)
        for h in range(heads):
            sk = slice(h * dk, (h + 1) * dk)
            sv = slice(h * dv, (h + 1) * dv)
            o = _chunk_head(tri, grp, b[:, sk], qs[:, sk], ks[:, sk], q_in[:, sk], k_out[:, sk],
                            hi_ref[rows, sv], c, groups, s_in(h), s_out(h), not has_state)
            y = _rms(o, gn_ref[...]) * _silu(hg_ref[rows, sv])
            out_ref[rows, sv] = (ya_ref[rows, sv] + _sigmoid(mb_ref[rows, sv]) * y).astype(BF16)
        return carry

    lax.fori_loop(0, chunks, chunk, 0)

    if not has_state:
        @pl.when(tb == n_tblk - 1)
        def _():
            for h in range(heads):
                so_ref[0, h] = s_scr[h].T


def _hg_branch(p, ya, lower_bounds, gn, state, merged_all, s_all, *, layer, row0, nb, t, c, heads, dk, dv, col,
               name):
    has_state = state is not None
    if has_state:
        assert t == c and BF16_ROWS % c == 0
        groups, chunks = BF16_ROWS // c, 1
    else:
        groups, chunks = 1, min(2, t // c)
    rblk = groups * c * chunks
    n_tblk = t // (c * chunks)
    width = heads * dk
    assert heads * dv == width and nb % groups == 0 and row0 % rblk == 0 and t % (c * chunks) == 0
    rb0 = row0 // rblk

    def rowspec(off):
        assert off % width == 0
        return pl.BlockSpec((rblk, width), lambda b, tb: (rb0 + b * n_tblk + tb, off // width))

    local = pl.BlockSpec((rblk, width), lambda b, tb: (b * n_tblk + tb, 0))
    depth = lower_bounds.shape[0]
    in_specs = [rowspec(col["hq"]), rowspec(col["hf"]), rowspec(col["hi"]), rowspec(col["hg"]),
                rowspec(col["mb"]), local,
                pl.BlockSpec((depth, width), lambda b, tb: (0, 0)),
                pl.BlockSpec((1, dv), lambda b, tb: (0, 0))]
    args = [p, p, p, p, p, ya, lower_bounds, gn]
    state_spec = pl.BlockSpec((None, groups, heads, dk, dv), lambda b, tb: (layer, b, 0, 0, 0))
    scratch = []
    if has_state:
        in_specs.append(state_spec)
        args.append(state)
    else:
        scratch = [pltpu.VMEM((heads, dv, dk), F32)]
    aliases = {}
    for out_idx, prior in enumerate((merged_all, s_all)):
        if prior is not None:
            aliases[len(args)] = out_idx
            in_specs.append(pl.BlockSpec(memory_space=pl.ANY))
            args.append(prior)
    merged, s_new = pl.pallas_call(
        functools.partial(_hg_kernel, layer=layer, c=c, groups=groups, chunks=chunks, n_tblk=n_tblk,
                          heads=heads, dk=dk, dv=dv, scale=dk ** -0.5, has_state=has_state,
                          n_aliased=len(aliases)),
        grid=(nb // groups, n_tblk),
        in_specs=in_specs,
        out_specs=[pl.BlockSpec((rblk, width), lambda b, tb: (rb0 + b * n_tblk + tb, 0)), state_spec],
        out_shape=[jax.ShapeDtypeStruct((p.shape[0], width), BF16),
                   jax.ShapeDtypeStruct((depth, nb, heads, dk, dv), F32)],
        input_output_aliases=aliases,
        scratch_shapes=scratch,
        compiler_params=_params(2),
        name=name,
    )(*args)
    return merged, s_new


def _up_kernel(x_ref, wa_ref, wg_ref, cwa_ref, cwg_ref, cba_ref, cbg_ref, h0a_ref, h0g_ref, h1a_ref, h1g_ref,
               h_ref, taila_ref, tailg_ref, usa_ref, usg_ref, sa_scr, sg_scr, wa_bf16, wg_bf16,
               *, tm, n_tiles, n_row_tiles, n_prompt_tiles, tiles_per_seq, ts):
    s = pl.program_id(0)
    hist = SUBLANES

    @pl.when(s == 0)
    def _():
        sa_scr[...] = jnp.zeros_like(sa_scr)
        sg_scr[...] = jnp.zeros_like(sg_scr)

    @pl.when(jnp.logical_and(s % n_row_tiles == 0, s < n_tiles))
    def _():
        wa_bf16[...] = wa_ref[...].astype(BF16)
        wg_bf16[...] = wg_ref[...].astype(BF16)

    ip = jnp.maximum(s - 1, 0) % n_row_tiles
    is_sample = ip >= n_prompt_tiles
    seq_start = jnp.logical_and(jnp.logical_not(is_sample), (ip % tiles_per_seq) == 0)
    row = lax.broadcasted_iota(jnp.int32, (tm, 1), 0)
    pos = jnp.where(is_sample, row & (ts - 1), jnp.where(seq_start, row, CONV_W))
    keep1 = pos >= 1
    keep2 = pos >= 2

    def epilogue(cw_ref, cb_ref, h0_ref, h1_ref, tail_ref, us_ref, scr):
        u = scr[hist:hist + tm, :]
        u1 = jnp.where(keep1, scr[hist - 1:hist - 1 + tm, :], 0.0)
        u2 = jnp.where(keep2, scr[hist - 2:hist - 2 + tm, :], 0.0)
        u1 = u1 + jnp.where(is_sample, h1_ref[...], 0.0)
        u2 = u2 + jnp.where(is_sample, h0_ref[...], 0.0)
        uc = u2 * cw_ref[0:1, :] + u1 * cw_ref[1:2, :] + u * cw_ref[2:3, :] + cb_ref[...]
        tail = scr[tm:tm + hist, :]
        scr[0:hist, :] = tail
        tail_ref[0] = tail
        us_ref[...] = u
        return uc

    a = epilogue(cwa_ref, cba_ref, h0a_ref, h1a_ref, taila_ref, usa_ref, sa_scr)
    g = epilogue(cwg_ref, cbg_ref, h0g_ref, h1g_ref, tailg_ref, usg_ref, sg_scr)
    h_ref[...] = (_silu(a) * g).astype(BF16)

    x = x_ref[...]
    sa_scr[hist:hist + tm, :] = jnp.dot(x, wa_bf16[...], preferred_element_type=F32)
    sg_scr[hist:hist + tm, :] = jnp.dot(x, wg_bf16[...], preferred_element_type=F32)


def _up_conv_gate(xb, w, layer, cw, cb, h0, h1, *, n_prompt, t_prompt, ts, name):
    n, d = xb.shape
    f = w.shape[2] // 2
    n_sample = n - n_prompt
    tm = _divisor(_gcd(t_prompt, n_sample), 512, BF16_ROWS)
    assert tm % ts == 0 and ts & (ts - 1) == 0 and ts >= CONV_W - 1 and tm >= SUBLANES
    tn = _divisor(f, 256, LANES)
    ncb = f // tn
    npt = n_prompt // tm
    nrt = n // tm
    n_tiles = ncb * nrt

    def cur(s):
        c = jnp.minimum(s, n_tiles - 1)
        return c // nrt, c % nrt

    def prev(s):
        c = jnp.maximum(s - 1, 0)
        return c // nrt, c % nrt

    def prev_sample(s, off):
        j, i = prev(s)
        return jnp.maximum(i - npt, 0), off + j

    in_specs = [pl.BlockSpec((tm, d), lambda s: (cur(s)[1], 0)),
                pl.BlockSpec((None, d, tn), lambda s: (layer, 0, cur(s)[0])),
                pl.BlockSpec((None, d, tn), lambda s: (layer, 0, ncb + cur(s)[0])),
                pl.BlockSpec((CONV_W, tn), lambda s: (0, prev(s)[0])),
                pl.BlockSpec((CONV_W, tn), lambda s: (0, ncb + prev(s)[0])),
                pl.BlockSpec((1, tn), lambda s: (0, prev(s)[0])),
                pl.BlockSpec((1, tn), lambda s: (0, ncb + prev(s)[0])),
                pl.BlockSpec((tm, tn), lambda s: prev_sample(s, 0)),
                pl.BlockSpec((tm, tn), lambda s: prev_sample(s, ncb)),
                pl.BlockSpec((tm, tn), lambda s: prev_sample(s, 0)),
                pl.BlockSpec((tm, tn), lambda s: prev_sample(s, ncb))]
    tail_spec = pl.BlockSpec((1, SUBLANES, tn), lambda s: (prev(s)[1], 0, prev(s)[0]))
    us_spec = pl.BlockSpec((tm, tn), lambda s: prev_sample(s, 0))
    out_specs = [pl.BlockSpec((tm, tn), lambda s: (prev(s)[1], prev(s)[0])), tail_spec, tail_spec,
                 us_spec, us_spec]
    out_shape = [jax.ShapeDtypeStruct((n, f), BF16),
                 jax.ShapeDtypeStruct((nrt, SUBLANES, f), F32), jax.ShapeDtypeStruct((nrt, SUBLANES, f), F32),
                 jax.ShapeDtypeStruct((n_sample, f), F32), jax.ShapeDtypeStruct((n_sample, f), F32)]
    return pl.pallas_call(
        functools.partial(_up_kernel, tm=tm, n_tiles=n_tiles, n_row_tiles=nrt, n_prompt_tiles=npt,
                          tiles_per_seq=t_prompt // tm, ts=ts),
        grid=(n_tiles + 1,),
        in_specs=in_specs,
        out_specs=out_specs,
        out_shape=out_shape,
        scratch_shapes=[pltpu.VMEM((tm + SUBLANES, tn), F32), pltpu.VMEM((tm + SUBLANES, tn), F32),
                        pltpu.VMEM((d, tn), BF16), pltpu.VMEM((d, tn), BF16)],
        compiler_params=_params(1),
        name=name,
    )(xb, w, w, cw, cw, cb, cb, h0, h0, h1, h1), tm


def _gcd(a, b):
    while b:
        a, b = b, a % b
    return a


def kernel(x_prompt, x_sample, state_gla, state_hgrn, state_ffn_conv, w_in, w_gla_gate2, b_gla_gate, g_gla_norm,
           lower_bounds, g_hgrn_norm, w_out, ln1_g, ln1_b, w_up, conv_w, conv_b, w_down, ln2_g, ln2_b):
    bp, tp, d = x_prompt.shape
    bs, ts, _ = x_sample.shape
    depth, _, heads, dk, dv = state_gla.shape
    _, _, hheads, hk, hv = state_hgrn.shape
    rank = w_gla_gate2.shape[1]
    f2 = w_up.shape[2]
    f = f2 // 2
    qk, vw, hgk, hgv = heads * dk, heads * dv, hheads * hk, hheads * hv
    c0 = 2 * qk + 2 * vw
    assert conv_w.shape[1] == CONV_W and rank <= LANES
    n_p, n_s = bp * tp, bs * ts
    alpha = (2.0 * depth) ** 0.25
    col = {"q": 0, "k": qk, "v": 2 * qk, "r": 2 * qk + vw,
           "hq": c0, "hf": c0 + hgk, "hi": c0 + 2 * hgk, "hg": c0 + 2 * hgk + hgv,
           "ma": c0 + 2 * hgk + 2 * hgv, "mb": c0 + 2 * hgk + 2 * hgv + d}
    gla_c_s = GLA_CHUNK if ts % GLA_CHUNK == 0 else ts
    hg_c_s = HG_CHUNK if ts % HG_CHUNK == 0 else ts
    gla_c_p = GLA_CHUNK if tp % GLA_CHUNK == 0 else tp
    hg_c_p = HG_CHUNK if tp % HG_CHUNK == 0 else tp

    x = (x_prompt.reshape(n_p, d), x_sample.reshape(n_s, d))
    xb = jnp.concatenate([x[0].astype(BF16), x[1].astype(BF16)], axis=0)
    gla_p = gla_s = hgr_p = hgr_s = None
    new_conv_p, new_conv_s = [], []
    w_main, w_lr = _pack_in_weights(w_in, c0, rank)
    w_down_bf16 = w_down.astype(BF16)
    for l in range(depth):
        wg2 = jnp.pad(w_gla_gate2[l], ((0, LANES - rank), (0, 0))).astype(BF16)
        p = _matmul(xb, w_main, l, tm_cap=1088, tn_cap=1024, w_is_nk=True, name=f"in_proj_{l}")
        lr = _matmul(xb, w_lr, l, tm_cap=1088, tn_cap=LANES, w_is_nk=True, name=f"in_proj_lr_{l}")

        gla_args = dict(depth=depth, layer=l, heads=heads, dk=dk, dv=dv, col=col)
        bg = b_gla_gate[l].reshape(1, qk)
        gn = g_gla_norm[l].reshape(1, dv)
        ya_p, gla_p = _gla_branch(p, lr, wg2, bg, gn, None, gla_p, row0=0, nb=bp, t=tp, c=gla_c_p,
                                  name=f"gla_prompt_{l}", **gla_args)
        ya_s, gla_s = _gla_branch(p, lr, wg2, bg, gn, state_gla, gla_s, row0=n_p, nb=bs, t=ts, c=gla_c_s,
                                  name=f"gla_sample_{l}", **gla_args)
        hg_args = dict(layer=l, heads=hheads, dk=hk, dv=hv, col=col)
        hgn = g_hgrn_norm[l].reshape(1, hv)
        merged, hgr_p = _hg_branch(p, ya_p, lower_bounds, hgn, None, None, hgr_p, row0=0, nb=bp, t=tp,
                                   c=hg_c_p, name=f"hgrn_prompt_{l}", **hg_args)
        merged, hgr_s = _hg_branch(p, ya_s, lower_bounds, hgn, state_hgrn, merged, hgr_s, row0=n_p, nb=bs,
                                   t=ts, c=hg_c_s, name=f"hgrn_sample_{l}", **hg_args)

        mix = _matmul(merged, w_out, l, tm_cap=1088, tn_cap=512, name=f"out_proj_{l}")
        x, xb = _residual_ln(x, mix, ln1_g[l], ln1_b[l], alpha, n_first=n_p, split_out=False, name=f"ln1_{l}")

        sc = state_ffn_conv[l]
        zero = jnp.zeros_like(sc[:, 0])
        h0 = jnp.stack([sc[:, 0], sc[:, 1]] + [zero] * (ts - 2), axis=1).reshape(n_s, f2)
        h1 = jnp.stack([sc[:, 1]] + [zero] * (ts - 1), axis=1).reshape(n_s, f2)
        (hb, tail_a, tail_g, us_a, us_g), tm = _up_conv_gate(
            xb, w_up, l, conv_w[l], conv_b[l].reshape(1, f2), h0, h1,
            n_prompt=n_p, t_prompt=tp, ts=ts, name=f"up_conv_{l}")
        last_tiles = (jnp.arange(bp) + 1) * (tp // tm) - 1
        keep = slice(SUBLANES - (CONV_W - 1), SUBLANES)
        new_conv_p.append(jnp.concatenate([tail_a[last_tiles][:, keep], tail_g[last_tiles][:, keep]], axis=-1))
        new_conv_s.append(jnp.concatenate([us_a.reshape(bs, ts, f)[:, ts - (CONV_W - 1):],
                                           us_g.reshape(bs, ts, f)[:, ts - (CONV_W - 1):]], axis=-1))

        y = _matmul(hb, w_down_bf16, l, tm_cap=512, tn_cap=512, name=f"down_proj_{l}")
        if l < depth - 1:
            x, xb = _residual_ln(x, y, ln2_g[l], ln2_b[l], alpha, n_first=n_p, split_out=False, name=f"ln2_{l}")
        else:
            y_prompt, y_sample = _residual_ln(x, y, ln2_g[l], ln2_b[l], alpha, n_first=n_p, split_out=True,
                                              name=f"ln2_{l}")

    return (y_prompt.reshape(bp, tp, d), y_sample.reshape(bs, ts, d), gla_p, gla_s, hgr_p, hgr_s,
            jnp.stack(new_conv_p), jnp.stack(new_conv_s))
```
